```python
import math
import jax
import jax.numpy as jnp
from jax import lax
import numpy as np

D_MODEL = 2048
BATCH = 4
SEQ = 4096
DEPTH = 4

GRID_W = 64
CTX_LEN = 256
N_MOD = 6

NA_HEADS = 8
NA_HEAD_DIM = 128
NA_WIDTH = NA_HEADS * NA_HEAD_DIM
NA_WIN_ROWS = 8
NA_WIN_COLS = 16

HY_WIDTH = D_MODEL - NA_WIDTH
HY_ORDER = 2
HY_DIRS = 2
HY_SHORT = 3
HY_EMB = 33
HY_BANDS = (HY_EMB - 1) // 2
HY_HIDDEN = 64
HY_DECAY_SHORT = 0.3
HY_DECAY_LONG = 1.5
HY_DECAY_TARGET = 1e-2
EVEN_IN = 3 * NA_WIDTH + (HY_ORDER + 1) * HY_WIDTH
EVEN_OUT = NA_WIDTH + HY_WIDTH

MLA_HEADS = 16
MLA_Q_LORA = 768
MLA_KV_LORA = 512
MLA_NOPE = 128
MLA_ROPE = 64
MLA_V = 128
MLA_QK = MLA_NOPE + MLA_ROPE
MLA_DOWN = MLA_Q_LORA + MLA_KV_LORA + MLA_ROPE
ROPE_BASE = 10000.0
Q_BLOCK = 128

N_EXPERTS = 16
EC_CAPACITY = 2
EXPERT_FF = 1024

EPS = 1e-6
NEG_INF = -1e30

kernel_name = 'hybrid_natten_hyena_mla_ecmoe_dit'


def rms_norm(x, g):
    xf = x.astype(jnp.float32)
    y = xf * lax.rsqrt(jnp.mean(xf * xf, axis=-1, keepdims=True) + EPS)
    return (y * g.astype(jnp.float32)).astype(x.dtype)


def modulate(h, shift, scale):
    return h * (1 + scale) + shift


def softmax_attend(q, k, v, scale):
    s = jnp.einsum('bqhd,bkhd->bhqk', q, k).astype(jnp.float32) * scale
    p = jax.nn.softmax(s, axis=-1).astype(v.dtype)
    return jnp.einsum('bhqk,bkhd->bqhd', p, v)


def blocked_attend(q, k, v, scale):
    b, s, h, dq = q.shape
    nblk = s // Q_BLOCK
    qb = jnp.moveaxis(q.reshape(b, nblk, Q_BLOCK, h, dq), 1, 0)
    ob = lax.map(lambda qi: softmax_attend(qi, k, v, scale), qb)
    return jnp.moveaxis(ob, 0, 1).reshape(b, s, h, v.shape[-1])


def grid_positions(n):
    t = jnp.arange(n)
    return t // GRID_W, t % GRID_W


def rope_1d(t, pos):
    n = t.shape[-1]
    inv = ROPE_BASE ** (-jnp.arange(0, n, 2, dtype=jnp.float32) / n)
    ang = pos.astype(jnp.float32)[:, None] * inv[None, :]
    ang = jnp.concatenate([ang, ang], axis=-1)[None, :, None, :]
    t1, t2 = jnp.split(t, 2, axis=-1)
    rot = jnp.concatenate([-t2, t1], axis=-1)
    return (t * jnp.cos(ang) + rot * jnp.sin(ang)).astype(t.dtype)


def rope_2d(t, rows, cols):
    tr, tc = jnp.split(t, 2, axis=-1)
    return jnp.concatenate([rope_1d(tr, rows), rope_1d(tc, cols)], axis=-1)


def neighbourhood_attention(q, k, v, k_ctx, v_ctx, rpb, scale):
    b, s, h, d = q.shape
    rows = s // GRID_W
    wr = min(NA_WIN_ROWS, rows)
    r_idx = jnp.arange(rows)
    row_start = jnp.clip(r_idx - wr // 2, 0, rows - wr)
    key_rows = row_start[:, None] + jnp.arange(wr)[None, :]
    cols = jnp.arange(GRID_W)
    col_start = jnp.clip(cols - NA_WIN_COLS // 2, 0, GRID_W - NA_WIN_COLS)
    in_win = (cols[None, :] >= col_start[:, None]) & (cols[None, :] < col_start[:, None] + NA_WIN_COLS)
    dr = key_rows - r_idx[:, None] + (NA_WIN_ROWS - 1)
    dc = jnp.clip(cols[None, :] - cols[:, None] + (NA_WIN_COLS - 1), 0, 2 * NA_WIN_COLS - 2)
    bias = rpb[:, dr[:, None, :, None], dc[None, :, None, :]].astype(jnp.float32)
    qg = q.reshape(b, rows, GRID_W, h, d)
    kg = k.reshape(b, rows, GRID_W, h, d)[:, key_rows]
    vg = v.reshape(b, rows, GRID_W, h, d)[:, key_rows]
    s_win = jnp.einsum('brqhd,brwkhd->bhrqwk', qg, kg).astype(jnp.float32) * scale + bias
    s_win = jnp.where(in_win[:, None, :], s_win, NEG_INF).reshape(b, h, rows, GRID_W, wr * GRID_W)
    s_ctx = jnp.einsum('brqhd,bchd->bhrqc', qg, k_ctx).astype(jnp.float32) * scale
    p = jax.nn.softmax(jnp.concatenate([s_win, s_ctx], axis=-1), axis=-1).astype(v.dtype)
    p_win = p[..., :wr * GRID_W].reshape(b, h, rows, GRID_W, wr, GRID_W)
    p_ctx = p[..., wr * GRID_W:]
    out = jnp.einsum('bhrqwk,brwkhd->brqhd', p_win, vg) + jnp.einsum('bhrqc,bchd->brqhd', p_ctx, v_ctx)
    return out.reshape(b, s, h * d)


def short_conv(u, w, bias):
    ch = u.shape[-1]
    y = lax.conv_general_dilated(u, w.astype(u.dtype)[:, None, :], window_strides=(1,),
                                 padding=((HY_SHORT // 2, HY_SHORT // 2),),
                                 dimension_numbers=('NWC', 'WIO', 'NWC'), feature_group_count=ch)
    return y + bias.astype(u.dtype)


def hyena_filters(length, w1, b1, w2, b2, w3, freq):
    f32 = jnp.float32
    t = jnp.linspace(0.0, 1.0, length, dtype=f32)[:, None]
    w = 2.0 * math.pi * jnp.arange(length, dtype=f32)[:, None] / length
    bands = jnp.linspace(1e-4, HY_BANDS - 1, HY_BANDS, dtype=f32)[None, :]
    z = jnp.concatenate([t, jnp.cos(bands * w), -jnp.sin(bands * w)], axis=-1)
    fr = freq.astype(f32)
    hid = jnp.sin(fr * (z @ w1.astype(f32) + b1.astype(f32)))
    hid = jnp.sin(fr * (hid @ w2.astype(f32) + b2.astype(f32)))
    filt = (hid @ w3.astype(f32)).reshape(length, HY_ORDER * HY_DIRS, HY_WIDTH)
    d_lo = math.log(HY_DECAY_TARGET) / HY_DECAY_LONG
    d_hi = math.log(HY_DECAY_TARGET) / HY_DECAY_SHORT
    deltas = jnp.abs(jnp.linspace(d_lo, d_hi, HY_WIDTH, dtype=f32))
    filt = filt * jnp.exp(-t * deltas)[:, None, :]
    return filt * lax.rsqrt(jnp.sum(filt * filt, axis=0, keepdims=True) + EPS)


def bidir_long_conv(u, h_fwd, h_bwd, d_skip):
    n = u.shape[1]
    nfft = 2 * n
    uf = u.astype(jnp.float32)
    hf = jnp.fft.rfft(h_fwd, n=nfft, axis=0)[None]
    hb = jnp.fft.rfft(h_bwd, n=nfft, axis=0)[None]
    y_f = jnp.fft.irfft(jnp.fft.rfft(uf, n=nfft, axis=1) * hf, n=nfft, axis=1)[:, :n]
    y_b = jnp.fft.irfft(jnp.fft.rfft(uf[:, ::-1], n=nfft, axis=1) * hb, n=nfft, axis=1)[:, :n][:, ::-1]
    return (y_f + y_b + uf * d_skip.astype(jnp.float32)).astype(u.dtype)


def hyena_operator(p, short_w, short_b, filt, d_skip):
    parts = jnp.split(short_conv(p, short_w, short_b), HY_ORDER + 1, axis=-1)
    z = parts[0]
    for o in range(HY_ORDER):
        z = parts[o + 1] * bidir_long_conv(z, filt[:, HY_DIRS * o], filt[:, HY_DIRS * o + 1], d_skip[o])
    return z


def na_heads(t):
    return t.reshape(t.shape[:-1] + (NA_HEADS, NA_HEAD_DIM))


def even_mixer(h_lat, h_ctx, w_in, w_out, q_g, k_g, rpb, short_w, short_b,
               f_w1, f_b1, f_w2, f_b2, f_w3, f_freq, hy_d, need_ctx):
    b, s, _ = h_lat.shape
    lc = h_ctx.shape[1]
    scale = NA_HEAD_DIM ** -0.5
    q_l, k_l, v_l, hy_lat = jnp.split(h_lat @ w_in, [NA_WIDTH, 2 * NA_WIDTH, 3 * NA_WIDTH], axis=-1)
    q_l = rms_norm(na_heads(q_l), q_g)
    k_l = rms_norm(na_heads(k_l), k_g)
    v_l = na_heads(v_l)
    if need_ctx:
        q_c, k_c, v_c, hy_ctx = jnp.split(h_ctx @ w_in, [NA_WIDTH, 2 * NA_WIDTH, 3 * NA_WIDTH], axis=-1)
        q_c = rms_norm(na_heads(q_c), q_g)
    else:
        k_c, v_c = jnp.split(h_ctx @ w_in[:, NA_WIDTH:3 * NA_WIDTH], 2, axis=-1)
    k_c = rms_norm(na_heads(k_c), k_g)
    v_c = na_heads(v_c)
    na_l = neighbourhood_attention(q_l, k_l, v_l, k_c, v_c, rpb, scale)
    hy_l = hyena_operator(hy_lat, short_w, short_b, hyena_filters(s, f_w1, f_b1, f_w2, f_b2, f_w3, f_freq), hy_d)
    y_lat = jnp.concatenate([na_l, hy_l], axis=-1) @ w_out
    y_ctx = None
    if need_ctx:
        na_c = softmax_attend(q_c, k_c, v_c, scale).reshape(b, lc, NA_WIDTH)
        hy_c = hyena_operator(hy_ctx, short_w, short_b, hyena_filters(lc, f_w1, f_b1, f_w2, f_b2, f_w3, f_freq), hy_d)
        y_ctx = jnp.concatenate([na_c, hy_c], axis=-1) @ w_out
    return y_lat, y_ctx


def mla_project(h, w_down, qa_g, kva_g, w_uq, w_ukv, q_g, k_g, with_q):
    b, l, _ = h.shape
    if with_q:
        q_a, kv_a, k_r = jnp.split(h @ w_down, [MLA_Q_LORA, MLA_Q_LORA + MLA_KV_LORA], axis=-1)
    else:
        kv_a, k_r = jnp.split(h @ w_down[:, MLA_Q_LORA:], [MLA_KV_LORA], axis=-1)
    kv = (rms_norm(kv_a, kva_g) @ w_ukv).reshape(b, l, MLA_HEADS, MLA_NOPE + MLA_V)
    k_nope, v = jnp.split(kv, [MLA_NOPE], axis=-1)
    k_r = jnp.broadcast_to(k_r[:, :, None, :], (b, l, MLA_HEADS, MLA_ROPE))
    k = rms_norm(jnp.concatenate([k_nope, k_r], axis=-1), k_g)
    q = None
    if with_q:
        q = rms_norm((rms_norm(q_a, qa_g) @ w_uq).reshape(b, l, MLA_HEADS, MLA_QK), q_g)
    return q, k, v


def rope_tail(t, rows, cols):
    return jnp.concatenate([t[..., :MLA_NOPE], rope_2d(t[..., MLA_NOPE:], rows, cols)], axis=-1)


def mla_mixer(h_lat, h_ctx, w_down, qa_g, kva_g, w_uq, w_ukv, q_g, k_g, w_o, need_ctx):
    b, s, _ = h_lat.shape
    lc = h_ctx.shape[1]
    rows, cols = grid_positions(s)
    scale = MLA_QK ** -0.5
    q_l, k_l, v_l = mla_project(h_lat, w_down, qa_g, kva_g, w_uq, w_ukv, q_g, k_g, True)
    q_l = rope_tail(q_l, rows, cols)
    k_l = rope_tail(k_l, rows, cols)
    q_c, k_c, v_c = mla_project(h_ctx, w_down, qa_g, kva_g, w_uq, w_ukv, q_g, k_g, need_ctx)
    k_all = jnp.concatenate([k_c, k_l], axis=1)
    v_all = jnp.concatenate([v_c, v_l], axis=1)
    y_lat = blocked_attend(q_l, k_all, v_all, scale).reshape(b, s, MLA_HEADS * MLA_V) @ w_o
    y_ctx = None
    if need_ctx:
        y_ctx = softmax_attend(q_c, k_c, v_c, scale).reshape(b, lc, MLA_HEADS * MLA_V) @ w_o
    return y_lat, y_ctx


def expert_choice_ffn(h, router_w, w_gate, w_up, w_down):
    b, n, d = h.shape
    cap = max(1, (EC_CAPACITY * n) // N_EXPERTS)
    aff = jax.nn.softmax((h @ router_w).astype(jnp.float32), axis=-1)
    gates, idx = lax.top_k(jnp.swapaxes(aff, 1, 2), cap)
    xs = jax.vmap(lambda hb, ib: hb[ib])(h, idx)
    a = jnp.einsum('becd,edf->becf', xs, w_gate)
    u = jnp.einsum('becd,edf->becf', xs, w_up)
    y = jnp.einsum('becf,efd->becd', jax.nn.silu(a) * u, w_down)
    y = y * gates[..., None].astype(y.dtype)
    return jax.vmap(lambda yb, ib: jax.ops.segment_sum(yb.reshape(-1, d), ib.reshape(-1), num_segments=n))(y, idx)


def setup_inputs(seed: int = 0) -> dict:
    key = jax.random.key(seed)
    ks = iter(jax.random.split(key, 34))
    n_ev = (DEPTH + 1) // 2
    n_od = DEPTH // 2
    d = D_MODEL

    def nrm(shape, scale):
        return jax.random.normal(next(ks), shape, jnp.float32) * scale

    def gain(shape):
        return 1.0 + nrm(shape, 0.05)

    return {
        'x': nrm((BATCH, SEQ, d), 1.0),
        'c': nrm((BATCH, d), 1.0),
        'ctx': nrm((BATCH, CTX_LEN, d), 1.0),
        'c_ctx': nrm((d,), 1.0),
        'ada_w': nrm((DEPTH, d, N_MOD * d), 0.5 * d ** -0.5),
        'ada_b': nrm((DEPTH, N_MOD * d), 0.02),
        'norm1_g': gain((DEPTH, d)),
        'norm2_g': gain((DEPTH, d)),
        'router_w': nrm((DEPTH, d, N_EXPERTS), d ** -0.5),
        'moe_w_gate': nrm((DEPTH, N_EXPERTS, d, EXPERT_FF), d ** -0.5),
        'moe_w_up': nrm((DEPTH, N_EXPERTS, d, EXPERT_FF), d ** -0.5),
        'moe_w_down': nrm((DEPTH, N_EXPERTS, EXPERT_FF, d), EXPERT_FF ** -0.5),
        'ev_w_in': nrm((n_ev, d, EVEN_IN), d ** -0.5),
        'ev_w_out': nrm((n_ev, EVEN_OUT, d), EVEN_OUT ** -0.5),
        'na_q_g': gain((n_ev, NA_HEAD_DIM)),
        'na_k_g': gain((n_ev, NA_HEAD_DIM)),
        'na_rpb': nrm((n_ev, NA_HEADS, 2 * NA_WIN_ROWS - 1, 2 * NA_WIN_COLS - 1), 0.1),
        'hy_short_w': nrm((n_ev, HY_SHORT, (HY_ORDER + 1) * HY_WIDTH), HY_SHORT ** -0.5),
        'hy_short_b': nrm((n_ev, (HY_ORDER + 1) * HY_WIDTH), 0.02),
        'hy_w1': nrm((n_ev, HY_EMB, HY_HIDDEN), HY_EMB ** -0.5),
        'hy_b1': nrm((n_ev, HY_HIDDEN), 0.2),
        'hy_w2': nrm((n_ev, HY_HIDDEN, HY_HIDDEN), HY_HIDDEN ** -0.5),
        'hy_b2': nrm((n_ev, HY_HIDDEN), 0.2),
        'hy_w3': nrm((n_ev, HY_HIDDEN, HY_ORDER * HY_DIRS * HY_WIDTH), HY_HIDDEN ** -0.5),
        'hy_freq': gain((n_ev, HY_HIDDEN)),
        'hy_d': nrm((n_ev, HY_ORDER, HY_WIDTH), 0.5),
        'mla_w_down': nrm((n_od, d, MLA_DOWN), d ** -0.5),
        'mla_qa_g': gain((n_od, MLA_Q_LORA)),
        'mla_kva_g': gain((n_od, MLA_KV_LORA)),
        'mla_w_uq': nrm((n_od, MLA_Q_LORA, MLA_HEADS * MLA_QK), MLA_Q_LORA ** -0.5),
        'mla_w_ukv': nrm((n_od, MLA_KV_LORA, MLA_HEADS * (MLA_NOPE + MLA_V)), MLA_KV_LORA ** -0.5),
        'mla_q_g': gain((n_od, MLA_QK)),
        'mla_k_g': gain((n_od, MLA_QK)),
        'mla_w_o': nrm((n_od, MLA_HEADS * MLA_V, d), (MLA_HEADS * MLA_V) ** -0.5),
    }


def reference(x, c, ctx, c_ctx, ada_w, ada_b, norm1_g, norm2_g, router_w, moe_w_gate, moe_w_up, moe_w_down,
              ev_w_in, ev_w_out, na_q_g, na_k_g, na_rpb, hy_short_w, hy_short_b, hy_w1, hy_b1, hy_w2, hy_b2,
              hy_w3, hy_freq, hy_d, mla_w_down, mla_qa_g, mla_kva_g, mla_w_uq, mla_w_ukv, mla_q_g, mla_k_g,
              mla_w_o):
    sc = jax.nn.silu(c)
    sc_ctx = jax.nn.silu(c_ctx)
    for i in range(DEPTH):
        last = i == DEPTH - 1
        j = i // 2
        m_l = [m[:, None, :] for m in jnp.split(sc @ ada_w[i] + ada_b[i], N_MOD, axis=-1)]
        m_c = jnp.split(sc_ctx @ ada_w[i] + ada_b[i], N_MOD, axis=-1)
        h_l = modulate(rms_norm(x, norm1_g[i]), m_l[0], m_l[1])
        h_c = modulate(rms_norm(ctx, norm1_g[i]), m_c[0], m_c[1])
        if i % 2 == 0:
            y_l, y_c = even_mixer(h_l, h_c, ev_w_in[j], ev_w_out[j], na_q_g[j], na_k_g[j], na_rpb[j],
                                  hy_short_w[j], hy_short_b[j], hy_w1[j], hy_b1[j], hy_w2[j], hy_b2[j],
                                  hy_w3[j], hy_freq[j], hy_d[j], not last)
        else:
            y_l, y_c = mla_mixer(h_l, h_c, mla_w_down[j], mla_qa_g[j], mla_kva_g[j], mla_w_uq[j],
                                 mla_w_ukv[j], mla_q_g[j], mla_k_g[j], mla_w_o[j], not last)
        x = x + m_l[2] * y_l
        x = x + m_l[5] * expert_choice_ffn(modulate(rms_norm(x, norm2_g[i]), m_l[3], m_l[4]),
                                           router_w[i], moe_w_gate[i], moe_w_up[i], moe_w_down[i])
        if not last:
            ctx = ctx + m_c[2] * y_c
            ctx = ctx + m_c[5] * expert_choice_ffn(modulate(rms_norm(ctx, norm2_g[i]), m_c[3], m_c[4]),
                                                   router_w[i], moe_w_gate[i], moe_w_up[i], moe_w_down[i])
    return x
```

```python
import functools
import math

import numpy as np
import jax
import jax.numpy as jnp
from jax import lax
from jax.experimental import pallas as pl
from jax.experimental.pallas import tpu as pltpu

F32, BF16, I32, U32 = jnp.float32, jnp.bfloat16, jnp.int32, jnp.uint32
HIGHEST = lax.Precision.HIGHEST

D_MODEL = 2048
DEPTH = 4
GRID_W = 64
N_MOD = 6
NA_HEADS = 8
NA_HEAD_DIM = 128
NA_WIDTH = NA_HEADS * NA_HEAD_DIM
NA_WIN_ROWS = 8
NA_WIN_COLS = 16
HY_WIDTH = D_MODEL - NA_WIDTH
HY_ORDER = 2
HY_EMB = 33
HY_BANDS = (HY_EMB - 1) // 2
HY_HIDDEN = 64
HY_DECAY_SHORT = 0.3
HY_DECAY_LONG = 1.5
HY_DECAY_TARGET = 1e-2
MLA_HEADS = 16
MLA_Q_LORA = 768
MLA_KV_LORA = 512
MLA_NOPE = 128
MLA_ROPE = 64
MLA_V = 128
MLA_QK = MLA_NOPE + MLA_ROPE
MLA_HEAD_PAD = 256
ROPE_BASE = 10000.0
N_EXPERTS = 16
EC_CAPACITY = 2
EXPERT_FF = 1024
EPS = 1e-6
NEG_INF = -1e30

LANES = 128
V7X_VMEM_LIMIT_BYTES = 56 * 1024 * 1024


def _params(n_axes):
    return pltpu.CompilerParams(dimension_semantics=("arbitrary",) * n_axes,
                                vmem_limit_bytes=V7X_VMEM_LIMIT_BYTES)


def _nt(a, b):
    return lax.dot_general(a, b, (((1,), (1,)), ((), ())), preferred_element_type=F32)


def _mm(a, b, *, tm, tn, out_dtype, epi=None, extras=(), order="gij", precision=None, name="mm"):
    a3, b3 = a.ndim == 3, b.ndim == 3
    G = a.shape[0] if a3 else (b.shape[0] if b3 else 1)
    M = a.shape[-2]
    K, N = b.shape[-2:]
    tm, tn = min(tm, M), min(tn, N)
    assert M % tm == 0 and N % tn == 0 and a.shape[-1] >= K
    sizes = dict(g=G, i=M // tm, j=N // tn)
    grid = tuple(sizes[c] for c in order)

    def gij(pid):
        d = dict(zip(order, pid))
        return d["g"], d["i"], d["j"]

    def a_map(*pid):
        g, i, _ = gij(pid)
        return (g, i, 0) if a3 else (i, 0)

    def b_map(*pid):
        g, _, j = gij(pid)
        return (g, 0, j) if b3 else (0, j)

    in_specs = [pl.BlockSpec((None, tm, K) if a3 else (tm, K), a_map),
                pl.BlockSpec((None, K, tn) if b3 else (K, tn), b_map)]
    for _, blk, fn in extras:
        in_specs.append(pl.BlockSpec(blk, lambda *pid, fn=fn: fn(*gij(pid))))

    def body(a_ref, b_ref, *rest):
        o_ref = rest[-1]
        acc = jnp.dot(a_ref[...], b_ref[...], preferred_element_type=F32, precision=precision)
        if epi is not None:
            acc = epi(acc, *rest[:-1])
        o_ref[...] = acc.astype(o_ref.dtype)

    return pl.pallas_call(
        body, grid=grid, in_specs=in_specs,
        out_specs=pl.BlockSpec((None, tm, tn), lambda *pid: gij(pid)),
        out_shape=jax.ShapeDtypeStruct((G, M, N), out_dtype),
        compiler_params=_params(3), name=name,
    )(a, b, *[e[0] for e in extras])


def _gated_residual_mm(a, w, res, gate, name):
    tm, tn = min(1024, a.shape[1]), min(1024, w.shape[1])
    return _mm(a, w, tm=tm, tn=tn, out_dtype=F32, name=name,
               epi=lambda acc, r_ref, g_ref: r_ref[...] + g_ref[...] * acc,
               extras=[(res, (None, tm, tn), lambda g, i, j: (g, i, j)),
                       (gate, (None, 1, tn), lambda g, i, j: (g, 0, j))])


def _ada_kernel(c_ref, w_ref, b_ref, o_ref):
    c = c_ref[...]
    sc = c * jax.nn.sigmoid(c)
    o_ref[...] = jnp.dot(sc, w_ref[...], preferred_element_type=F32, precision=HIGHEST) + b_ref[...]


def _ada_modulation(c8, ada_w, ada_b):
    depth, d, n = ada_w.shape
    tn = 1024
    return pl.pallas_call(
        _ada_kernel, grid=(depth, n // tn),
        in_specs=[pl.BlockSpec((8, d), lambda l, j: (0, 0)),
                  pl.BlockSpec((None, d, tn), lambda l, j: (l, 0, j)),
                  pl.BlockSpec((None, 1, tn), lambda l, j: (l, 0, j))],
        out_specs=pl.BlockSpec((None, 8, tn), lambda l, j: (l, 0, j)),
        out_shape=jax.ShapeDtypeStruct((depth, 8, n), F32),
        compiler_params=_params(2), name="ada_modulation",
    )(c8, ada_w, ada_b.reshape(depth, 1, n))


def _normmod_kernel(x_ref, g_ref, sh_ref, sc_ref, *rest, with_router):
    x = x_ref[...]
    y = x * lax.rsqrt(jnp.mean(x * x, axis=-1, keepdims=True) + EPS) * g_ref[...]
    h = y * (1.0 + sc_ref[...]) + sh_ref[...]
    if with_router:
        rw_ref, o_ref, lg_ref = rest
        lg_ref[...] = lax.dot_general(rw_ref[...], h, (((1,), (1,)), ((), ())),
                                      preferred_element_type=F32, precision=HIGHEST)
    else:
        (o_ref,) = rest
    o_ref[...] = h.astype(o_ref.dtype)


def _normmod(x, g, shift, scale, router_wt=None):
    G, M, D = x.shape
    tm = min(512, M)
    with_router = router_wt is not None
    in_specs = [pl.BlockSpec((None, tm, D), lambda g_, i: (g_, i, 0)),
                pl.BlockSpec((1, D), lambda g_, i: (0, 0)),
                pl.BlockSpec((None, 1, D), lambda g_, i: (g_, 0, 0)),
                pl.BlockSpec((None, 1, D), lambda g_, i: (g_, 0, 0))]
    out_specs = [pl.BlockSpec((None, tm, D), lambda g_, i: (g_, i, 0))]
    out_shape = [jax.ShapeDtypeStruct((G, M, D), BF16)]
    args = [x, g.reshape(1, D), shift, scale]
    if with_router:
        E = router_wt.shape[0]
        in_specs.append(pl.BlockSpec((E, D), lambda g_, i: (0, 0)))
        out_specs.append(pl.BlockSpec((None, E, tm), lambda g_, i: (g_, 0, i)))
        out_shape.append(jax.ShapeDtypeStruct((G, E, M), F32))
        args.append(router_wt)
    out = pl.pallas_call(
        functools.partial(_normmod_kernel, with_router=with_router), grid=(G, M // tm),
        in_specs=in_specs, out_specs=out_specs, out_shape=out_shape,
        compiler_params=_params(2), name="normmod_router" if with_router else "normmod",
    )(*args)
    return out if with_router else out[0]


def _headnorm_kernel(x_ref, g_ref, o_ref, *, n_heads, head_dim, scale):
    for h in range(n_heads):
        sl = slice(h * head_dim, (h + 1) * head_dim)
        x = x_ref[:, sl].astype(F32)
        y = x * lax.rsqrt(jnp.mean(x * x, axis=-1, keepdims=True) + EPS) * g_ref[...]
        o_ref[:, sl] = (y * scale).astype(o_ref.dtype)


def _headnorm(x, lane_block, g, scale):
    G, M, _ = x.shape
    tm = min(512, M)
    return pl.pallas_call(
        functools.partial(_headnorm_kernel, n_heads=NA_HEADS, head_dim=NA_HEAD_DIM, scale=scale),
        grid=(G, M // tm),
        in_specs=[pl.BlockSpec((None, tm, NA_WIDTH), lambda g_, i: (g_, i, lane_block)),
                  pl.BlockSpec((1, NA_HEAD_DIM), lambda g_, i: (0, 0))],
        out_specs=pl.BlockSpec((None, tm, NA_WIDTH), lambda g_, i: (g_, i, 0)),
        out_shape=jax.ShapeDtypeStruct((G, M, NA_WIDTH), BF16),
        compiler_params=_params(2), name="headnorm",
    )(x, g.reshape(1, NA_HEAD_DIM))


def _attn_kernel(*refs, two):
    if two:
        q_ref, k1_ref, v1_ref, k2_ref, v2_ref, o_ref = refs
    else:
        q_ref, k1_ref, v1_ref, o_ref = refs
    q = q_ref[...]
    s1 = _nt(q, k1_ref[...])
    m = jnp.max(s1, axis=-1, keepdims=True)
    if two:
        s2 = _nt(q, k2_ref[...])
        m = jnp.maximum(m, jnp.max(s2, axis=-1, keepdims=True))
    p1 = jnp.exp(s1 - m)
    l = jnp.sum(p1, axis=-1, keepdims=True)
    o = jnp.dot(p1.astype(BF16), v1_ref[...], preferred_element_type=F32)
    if two:
        p2 = jnp.exp(s2 - m)
        l = l + jnp.sum(p2, axis=-1, keepdims=True)
        o = o + jnp.dot(p2.astype(BF16), v2_ref[...], preferred_element_type=F32)
    o_ref[...] = (o / l).astype(o_ref.dtype)


def _attention(q, kv_pairs, *, n_heads, dq, dv, v_block, tq, name):
    G, Lq, _ = q.shape
    tq = min(tq, Lq)
    in_specs = [pl.BlockSpec((None, tq, dq), lambda g, h, i: (g, i, h))]
    args = [q]
    for k, v in kv_pairs:
        lk = k.shape[1]
        in_specs.append(pl.BlockSpec((None, lk, dq), lambda g, h, i: (g, 0, h)))
        in_specs.append(pl.BlockSpec((None, lk, dv), lambda g, h, i: (g, 0, v_block(h))))
        args += [k, v]
    return pl.pallas_call(
        functools.partial(_attn_kernel, two=len(kv_pairs) == 2),
        grid=(G, n_heads, Lq // tq), in_specs=in_specs,
        out_specs=pl.BlockSpec((None, tq, dv), lambda g, h, i: (g, i, h)),
        out_shape=jax.ShapeDtypeStruct((G, Lq, n_heads * dv), BF16),
        compiler_params=_params(3), name=name,
    )(*args)


NA_ROWS_PER_STEP = 8


def _na_kernel(q_ref, k_ref, v_ref, kc_ref, vc_ref, bias_ref, o_ref, *, n_rows):
    rb = pl.program_id(2)
    kc = kc_ref[...]
    vc = vc_ref[...]
    win = NA_WIN_ROWS * GRID_W
    for i in range(NA_ROWS_PER_STEP):
        r = rb * NA_ROWS_PER_STEP + i
        r0 = jnp.clip(r - NA_WIN_ROWS // 2, 0, n_rows - NA_WIN_ROWS)
        start = pl.multiple_of(r0 * GRID_W, GRID_W)
        q = q_ref[i * GRID_W:(i + 1) * GRID_W, :]
        s_w = _nt(q, k_ref[pl.ds(start, win), :]) + bias_ref[r - r0]
        s_c = _nt(q, kc)
        m = jnp.maximum(jnp.max(s_w, axis=-1, keepdims=True), jnp.max(s_c, axis=-1, keepdims=True))
        p_w = jnp.exp(s_w - m)
        p_c = jnp.exp(s_c - m)
        l = jnp.sum(p_w, axis=-1, keepdims=True) + jnp.sum(p_c, axis=-1, keepdims=True)
        o = (jnp.dot(p_w.astype(BF16), v_ref[pl.ds(start, win), :], preferred_element_type=F32)
             + jnp.dot(p_c.astype(BF16), vc, preferred_element_type=F32))
        o_ref[i * GRID_W:(i + 1) * GRID_W, :] = (o / l).astype(o_ref.dtype)


def _na_attention(q, k, proj, kc, proj_c, bias):
    G, S, _ = q.shape
    lc = kc.shape[1]
    n_rows = S // GRID_W
    assert n_rows >= NA_WIN_ROWS and n_rows % NA_ROWS_PER_STEP == 0
    tq = NA_ROWS_PER_STEP * GRID_W
    hd = NA_HEAD_DIM
    v_off = 2 * NA_HEADS
    return pl.pallas_call(
        functools.partial(_na_kernel, n_rows=n_rows),
        grid=(G, NA_HEADS, n_rows // NA_ROWS_PER_STEP),
        in_specs=[pl.BlockSpec((None, tq, hd), lambda g, h, r: (g, r, h)),
                  pl.BlockSpec((None, S, hd), lambda g, h, r: (g, 0, h)),
                  pl.BlockSpec((None, S, hd), lambda g, h, r: (g, 0, v_off + h)),
                  pl.BlockSpec((None, lc, hd), lambda g, h, r: (g, 0, h)),
                  pl.BlockSpec((None, lc, hd), lambda g, h, r: (g, 0, v_off + h)),
                  pl.BlockSpec((None, NA_WIN_ROWS, GRID_W, NA_WIN_ROWS * GRID_W),
                               lambda g, h, r: (h, 0, 0, 0))],
        out_specs=pl.BlockSpec((None, tq, hd), lambda g, h, r: (g, r, h)),
        out_shape=jax.ShapeDtypeStruct((G, S, NA_WIDTH), BF16),
        compiler_params=_params(3), name="na_attention",
    )(q, k, proj, kc, proj_c, bias)


def _na_bias(rpb):
    cols = np.arange(GRID_W)
    col_start = np.clip(cols - NA_WIN_COLS // 2, 0, GRID_W - NA_WIN_COLS)
    in_win = (cols[None, :] >= col_start[:, None]) & (cols[None, :] < col_start[:, None] + NA_WIN_COLS)
    dc = np.clip(cols[None, :] - cols[:, None] + (NA_WIN_COLS - 1), 0, 2 * NA_WIN_COLS - 2)
    w = np.arange(NA_WIN_ROWS)
    dr = w[None, :] - w[:, None] + (NA_WIN_ROWS - 1)
    b = rpb[:, dr[:, None, :, None], dc[None, :, None, :]].astype(F32)
    b = jnp.where(in_win[None, None, :, None, :], b, NEG_INF)
    return b.reshape(rpb.shape[0], NA_WIN_ROWS, GRID_W, NA_WIN_ROWS * GRID_W)


def _shortconv_kernel(x_ref, w_ref, b_ref, o_ref):
    x = x_ref[...].astype(F32)
    n = x.shape[0]
    row = lax.broadcasted_iota(I32, x.shape, 0)
    prev = jnp.where(row == 0, 0.0, pltpu.roll(x, 1, 0))
    nxt = jnp.where(row == n - 1, 0.0, pltpu.roll(x, n - 1, 0))
    w = w_ref[...]
    o_ref[...] = (w[0:1] * prev + w[1:2] * x + w[2:3] * nxt + b_ref[...]).astype(o_ref.dtype)


def _shortconv(proj, lane_off, w, b):
    G, L, _ = proj.shape
    C = w.shape[1]
    tc = 256
    off = lane_off // tc
    return pl.pallas_call(
        _shortconv_kernel, grid=(G, C // tc),
        in_specs=[pl.BlockSpec((None, L, tc), lambda g, j: (g, 0, off + j)),
                  pl.BlockSpec((3, tc), lambda g, j: (0, j)),
                  pl.BlockSpec((1, tc), lambda g, j: (0, j))],
        out_specs=pl.BlockSpec((None, L, tc), lambda g, j: (g, 0, j)),
        out_shape=jax.ShapeDtypeStruct((G, L, C), BF16),
        compiler_params=_params(2), name="hyena_shortconv",
    )(proj, w, b.reshape(1, C))


def _hyfilt_kernel(z_ref, w1_ref, b1_ref, w2_ref, b2_ref, fr_ref, w3f_ref, w3b_ref, dl_ref, os_ref, od_ref):
    z = z_ref[...]
    fr = fr_ref[...]
    hid = jnp.sin(fr * (jnp.dot(z, w1_ref[...], preferred_element_type=F32, precision=HIGHEST) + b1_ref[...]))
    hid = jnp.sin(fr * (jnp.dot(hid, w2_ref[...], preferred_element_type=F32, precision=HIGHEST) + b2_ref[...]))
    decay = jnp.exp(-z[:, 0:1] * dl_ref[...])

    def filt(w3_ref):
        f = jnp.dot(hid, w3_ref[...], preferred_element_type=F32, precision=HIGHEST) * decay
        return f * lax.rsqrt(jnp.sum(f * f, axis=0, keepdims=True) + EPS)

    ff, fb = filt(w3f_ref), filt(w3b_ref)
    os_ref[...] = (ff + fb).astype(os_ref.dtype)
    od_ref[...] = (ff - fb).astype(od_ref.dtype)


def _hyena_filters(L, w1, b1, w2, b2, w3, freq):
    t = jnp.linspace(0.0, 1.0, L, dtype=F32)[:, None]
    wv = 2.0 * math.pi * jnp.arange(L, dtype=F32)[:, None] / L
    bands = jnp.linspace(1e-4, HY_BANDS - 1, HY_BANDS, dtype=F32)[None, :]
    z = jnp.concatenate([t, jnp.cos(bands * wv), -jnp.sin(bands * wv)], axis=-1)
    z = jnp.pad(z, ((0, 0), (0, LANES - HY_EMB)))
    hp = LANES - HY_HIDDEN
    w1p = jnp.pad(w1, ((0, LANES - HY_EMB), (0, hp)))
    w2p = jnp.pad(w2, ((0, hp), (0, hp)))
    w3p = jnp.pad(w3, ((0, hp), (0, 0)))
    b1p = jnp.pad(b1, (0, hp)).reshape(1, LANES)
    b2p = jnp.pad(b2, (0, hp)).reshape(1, LANES)
    frp = jnp.pad(freq, (0, hp)).reshape(1, LANES)
    d_lo = math.log(HY_DECAY_TARGET) / HY_DECAY_LONG
    d_hi = math.log(HY_DECAY_TARGET) / HY_DECAY_SHORT
    deltas = jnp.abs(jnp.linspace(d_lo, d_hi, HY_WIDTH, dtype=F32)).reshape(1, HY_WIDTH)
    tc = 256
    nct = HY_WIDTH // tc
    const = lambda o, j: (0, 0)
    out = jax.ShapeDtypeStruct((L, HY_ORDER * HY_WIDTH), BF16)
    return pl.pallas_call(
        _hyfilt_kernel, grid=(HY_ORDER, nct),
        in_specs=[pl.BlockSpec((L, LANES), const), pl.BlockSpec((LANES, LANES), const),
                  pl.BlockSpec((1, LANES), const), pl.BlockSpec((LANES, LANES), const),
                  pl.BlockSpec((1, LANES), const), pl.BlockSpec((1, LANES), const),
                  pl.BlockSpec((LANES, tc), lambda o, j: (0, 2 * o * nct + j)),
                  pl.BlockSpec((LANES, tc), lambda o, j: (0, (2 * o + 1) * nct + j)),
                  pl.BlockSpec((1, tc), lambda o, j: (0, j))],
        out_specs=[pl.BlockSpec((L, tc), lambda o, j: (0, o * nct + j))] * 2,
        out_shape=[out, out], compiler_params=_params(2), name="hyena_filters",
    )(z, w1p, b1p, w2p, b2p, frp, w3p, w3p, deltas)


def _dft_tables(L):
    A = 1 << (int(math.log2(L)) // 2)
    period = 4 * L
    k = np.arange(L, dtype=np.int64)[:, None]
    ang_a = 2.0 * np.pi * (((2 * k + 1) * (A * np.arange(L // A, dtype=np.int64)[None, :])) % period) / period
    ang_b = 2.0 * np.pi * (((2 * k + 1) * np.arange(A, dtype=np.int64)[None, :]) % period) / period
    return tuple(np.asarray(f(x), np.float32) for x in (ang_a, ang_b) for f in (np.cos, np.sin))


def _dft_mats(L):
    ca, sa, cb, sb = (jnp.asarray(t) for t in _dft_tables(L))
    fc = (ca[:, :, None] * cb[:, None, :] - sa[:, :, None] * sb[:, None, :]).reshape(L, L)
    fs = (sa[:, :, None] * cb[:, None, :] + ca[:, :, None] * sb[:, None, :]).reshape(L, L)
    cat, sat, cbt, sbt = ca.T, sa.T, cb.T, sb.T
    fct = (cat[:, None, :] * cbt[None, :, :] - sat[:, None, :] * sbt[None, :, :]).reshape(L, L)
    fst = (sat[:, None, :] * cbt[None, :, :] + cat[:, None, :] * sbt[None, :, :]).reshape(L, L)
    return tuple(m.astype(BF16) for m in (fc, fs, fct, fst))


def _dft_fwd_kernel(fc_ref, fs_ref, u_ref, hc_ref, hs_ref, yc_ref, ys_ref):
    u = u_ref[...]
    xc = jnp.dot(fc_ref[...], u, preferred_element_type=F32)
    xs = jnp.dot(fs_ref[...], u, preferred_element_type=F32)
    hc, hs = hc_ref[...], hs_ref[...]
    yc_ref[...] = (xc * hc - xs * hs).astype(yc_ref.dtype)
    ys_ref[...] = (xc * hs + xs * hc).astype(ys_ref.dtype)


def _dft_fwd(fc, fs, u, u_block, hc, hs, h_block):
    G, L, _ = u.shape
    C = HY_WIDTH
    tr, tc = min(512, L), 512
    nct = C // tc
    spec_f = pl.BlockSpec((tr, L), lambda k, g, j: (k, 0))
    spec_h = pl.BlockSpec((None, tr, tc), lambda k, g, j: (0, k, h_block * nct + j))
    spec_y = pl.BlockSpec((None, tr, tc), lambda k, g, j: (g, k, j))
    out = jax.ShapeDtypeStruct((G, L, C), BF16)
    return pl.pallas_call(
        _dft_fwd_kernel, grid=(L // tr, G, nct),
        in_specs=[spec_f, spec_f, pl.BlockSpec((None, L, tc), lambda k, g, j: (g, 0, u_block * nct + j)),
                  spec_h, spec_h],
        out_specs=[spec_y, spec_y], out_shape=[out, out],
        compiler_params=_params(3), name="hyena_dft_fwd",
    )(fc, fs, u, hc, hs)


def _dft_inv_kernel(fct_ref, fst_ref, yc_ref, ys_ref, u_ref, x_ref, d_ref, o_ref, *, inv_scale):
    y = (jnp.dot(fct_ref[...], yc_ref[...], preferred_element_type=F32)
         + jnp.dot(fst_ref[...], ys_ref[...], preferred_element_type=F32)) * inv_scale
    y = y + u_ref[...].astype(F32) * d_ref[...]
    o_ref[...] = (x_ref[...].astype(F32) * y).astype(o_ref.dtype)


def _dft_inv(fct, fst, yc, ys, u, u_block, gate, gate_block, d):
    G, L, C = yc.shape
    tr, tc = min(512, L), 512
    nct = C // tc
    spec_f = pl.BlockSpec((tr, L), lambda n, g, j: (n, 0))
    spec_y = pl.BlockSpec((None, L, tc), lambda n, g, j: (g, 0, j))
    return pl.pallas_call(
        functools.partial(_dft_inv_kernel, inv_scale=1.0 / L),
        grid=(L // tr, G, nct),
        in_specs=[spec_f, spec_f, spec_y, spec_y,
                  pl.BlockSpec((None, tr, tc), lambda n, g, j: (g, n, u_block * nct + j)),
                  pl.BlockSpec((None, tr, tc), lambda n, g, j: (g, n, gate_block * nct + j)),
                  pl.BlockSpec((1, tc), lambda n, g, j: (0, j))],
        out_specs=pl.BlockSpec((None, tr, tc), lambda n, g, j: (g, n, j)),
        out_shape=jax.ShapeDtypeStruct((G, L, C), BF16),
        compiler_params=_params(3), name="hyena_dft_inv",
    )(fct, fst, yc, ys, u, gate, d.reshape(1, C))


def _hyena(proj, short_w, short_b, hsum, hdiff, hy_d, mats):
    fc, fs, fct, fst = mats
    sc = _shortconv(proj, 3 * NA_WIDTH, short_w, short_b)
    hc = _mm(fc, hsum, tm=512, tn=512, out_dtype=F32, name="hyena_filter_dft")
    hs = _mm(fs, hdiff, tm=512, tn=512, out_dtype=F32, name="hyena_filter_dft")
    z, z_block = sc, 0
    for o in range(HY_ORDER):
        yc, ys = _dft_fwd(fc, fs, z, z_block, hc, hs, o)
        z = _dft_inv(fct, fst, yc, ys, z, z_block, sc, o + 1, hy_d[o])
        z_block = 0
    return z


def _rope_rotate(y):
    lane = lax.broadcasted_iota(I32, y.shape, 1)
    first = (lane % 32) < 16
    return jnp.where(first, -pltpu.roll(y, LANES - 16, 1), pltpu.roll(y, 16, 1))


def _mla_qprep_kernel(x_ref, g_ref, cos_ref, sin_ref, o_ref, *, scale):
    cos, sin = cos_ref[...], sin_ref[...]
    g = g_ref[...]
    for h in range(MLA_HEADS):
        lo = h * MLA_HEAD_PAD
        xn = x_ref[:, lo:lo + LANES].astype(F32)
        xr = x_ref[:, lo + LANES:lo + 2 * LANES].astype(F32)
        ss = jnp.sum(xn * xn, axis=-1, keepdims=True) + jnp.sum(xr * xr, axis=-1, keepdims=True)
        rs = lax.rsqrt(ss * (1.0 / MLA_QK) + EPS) * scale
        yr = xr * rs * g[:, LANES:]
        o_ref[:, lo:lo + LANES] = (xn * rs * g[:, :LANES]).astype(o_ref.dtype)
        o_ref[:, lo + LANES:lo + 2 * LANES] = (yr * cos + _rope_rotate(yr) * sin).astype(o_ref.dtype)


def _mla_kprep_kernel(kv_ref, kr_ref, g_ref, cos_ref, sin_ref, o_ref):
    cos, sin = cos_ref[...], sin_ref[...]
    g = g_ref[...]
    kr = kr_ref[...].astype(F32)
    sr = jnp.sum(kr * kr, axis=-1, keepdims=True)
    krg = kr * g[:, LANES:]
    krot = krg * cos + _rope_rotate(krg) * sin
    for h in range(MLA_HEADS):
        lo = h * MLA_HEAD_PAD
        kn = kv_ref[:, lo:lo + LANES].astype(F32)
        rs = lax.rsqrt((jnp.sum(kn * kn, axis=-1, keepdims=True) + sr) * (1.0 / MLA_QK) + EPS)
        o_ref[:, lo:lo + LANES] = (kn * rs * g[:, :LANES]).astype(o_ref.dtype)
        o_ref[:, lo + LANES:lo + 2 * LANES] = (krot * rs).astype(o_ref.dtype)


def _mla_prep(x, kr, g_pad, cos, sin, scale=None):
    G, M, W = x.shape
    tm = min(256, M)
    row = lambda g_, i: (g_, i, 0)
    tab = pl.BlockSpec((tm, LANES), lambda g_, i: (i, 0))
    gsp = pl.BlockSpec((1, MLA_HEAD_PAD), lambda g_, i: (0, 0))
    xsp = pl.BlockSpec((None, tm, W), row)
    if kr is None:
        kern, in_specs, args = functools.partial(_mla_qprep_kernel, scale=scale), [xsp, gsp, tab, tab], [x, g_pad, cos, sin]
    else:
        kr_block = MLA_KV_LORA // LANES
        kern = _mla_kprep_kernel
        in_specs = [xsp, pl.BlockSpec((None, tm, LANES), lambda g_, i: (g_, i, kr_block)), gsp, tab, tab]
        args = [x, kr, g_pad, cos, sin]
    return pl.pallas_call(
        kern, grid=(G, M // tm), in_specs=in_specs, out_specs=pl.BlockSpec((None, tm, W), row),
        out_shape=jax.ShapeDtypeStruct((G, M, W), BF16), compiler_params=_params(2),
        name="mla_qprep" if kr is None else "mla_kprep",
    )(*args)


def _rope_tables(S):
    t = jnp.arange(S)
    half = MLA_ROPE // 2
    inv = ROPE_BASE ** (-jnp.arange(0, half, 2, dtype=F32) / half)
    ang_r = (t // GRID_W).astype(F32)[:, None] * inv[None, :]
    ang_c = (t % GRID_W).astype(F32)[:, None] * inv[None, :]
    ang = jnp.concatenate([ang_r, ang_r, ang_c, ang_c], axis=-1)
    pad = LANES - MLA_ROPE
    return (jnp.pad(jnp.cos(ang), ((0, 0), (0, pad)), constant_values=1.0),
            jnp.pad(jnp.sin(ang), ((0, 0), (0, pad))))


def _lane_cumsum(x):
    n = x.shape[1]
    tri = (lax.broadcasted_iota(I32, (LANES, LANES), 0) <= lax.broadcasted_iota(I32, (LANES, LANES), 1))
    tri = jnp.where(tri, 1.0, 0.0).astype(BF16)
    off = jnp.zeros((x.shape[0], 1), F32)
    outs = []
    for j in range(n // LANES):
        c = jnp.dot(x[:, j * LANES:(j + 1) * LANES], tri, preferred_element_type=F32) + off
        outs.append(c)
        off = c[:, LANES - 1:LANES]
    return jnp.concatenate(outs, axis=1)


ROUTE_SLOT_CHUNK = 64


def _route_kernel(lg_ref, idx_ref, gt_ref, aff_s, slot_s, *, cap, chunk):
    lg = lg_ref[...]
    n_exp, n_tok = lg.shape
    ex = jnp.exp(lg - jnp.max(lg, axis=0, keepdims=True))
    aff = ex / jnp.sum(ex, axis=0, keepdims=True)
    bits = pltpu.bitcast(aff, I32)

    def search(i, v):
        cand = v | jnp.left_shift(jnp.int32(1), 30 - i)
        cnt = jnp.sum(jnp.where(bits >= cand, 1.0, 0.0), axis=1, keepdims=True)
        return jnp.where(cnt >= cap, cand, v)

    thr = lax.fori_loop(0, 31, search, jnp.zeros((n_exp, 1), I32))
    gt = bits > thr
    eq = bits == thr
    need = cap - jnp.sum(jnp.where(gt, 1.0, 0.0), axis=1, keepdims=True)
    eq_rank = _lane_cumsum(jnp.where(eq, 1.0, 0.0).astype(BF16))
    sel = jnp.where(gt, 1.0, jnp.where(eq, jnp.where(eq_rank <= need, 1.0, 0.0), 0.0))
    slot = _lane_cumsum(sel.astype(BF16)) * sel
    aff_s[...] = aff
    slot_s[...] = slot
    tok = lax.broadcasted_iota(I32, (chunk, n_tok), 1).astype(F32)

    def per_expert(e, carry):
        srow = slot_s[pl.ds(e, 1), :]
        arow = aff_s[pl.ds(e, 1), :]

        def per_chunk(c, carry2):
            p0 = pl.multiple_of(c * chunk, chunk)
            want = (lax.broadcasted_iota(I32, (chunk, 1), 0) + (p0 + 1)).astype(F32)
            hit = srow == want
            idx_ref[e, pl.ds(p0, chunk), :] = jnp.sum(
                jnp.where(hit, tok, 0.0), axis=1, keepdims=True).astype(I32)
            gt_ref[e, pl.ds(p0, chunk), :] = jnp.sum(
                jnp.where(hit, arow, 0.0), axis=1, keepdims=True)
            return carry2

        return lax.fori_loop(0, cap // chunk, per_chunk, carry)

    lax.fori_loop(0, n_exp, per_expert, 0)


def _route(logits_t, cap):
    G, E, N = logits_t.shape
    chunk = min(ROUTE_SLOT_CHUNK, cap)
    assert cap % chunk == 0 and N % LANES == 0
    kern = functools.partial(_route_kernel, cap=cap, chunk=chunk)
    spec_o = pl.BlockSpec((None, E, cap, 1), lambda g: (g, 0, 0, 0))
    return pl.pallas_call(
        kern, grid=(G,), in_specs=[pl.BlockSpec((None, E, N), lambda g: (g, 0, 0))],
        out_specs=[spec_o, spec_o],
        out_shape=[jax.ShapeDtypeStruct((G, E, cap, 1), I32), jax.ShapeDtypeStruct((G, E, cap, 1), F32)],
        scratch_shapes=[pltpu.VMEM((E, N), F32), pltpu.VMEM((E, N), F32)],
        compiler_params=_params(1), name="moe_route",
    )(logits_t)


def _gather_kernel(idx_ref, h_ref, o_ref, *, cap, n_exp):
    base = (pl.program_id(0) * n_exp + pl.program_id(1)) * cap

    def body(p, carry):
        r = idx_ref[base + p]
        o_ref[pl.ds(p, 1), :] = h_ref[pl.ds(r, 1), :]
        return carry

    lax.fori_loop(0, cap, body, 0, unroll=8)


def _gather(idx_flat, h32, n_exp, cap):
    G, N, W = h32.shape
    return pl.pallas_call(
        functools.partial(_gather_kernel, cap=cap, n_exp=n_exp),
        grid_spec=pltpu.PrefetchScalarGridSpec(
            num_scalar_prefetch=1, grid=(G, n_exp),
            in_specs=[pl.BlockSpec((None, N, W), lambda g, e, idx: (g, 0, 0))],
            out_specs=pl.BlockSpec((None, None, cap, W), lambda g, e, idx: (g, e, 0, 0))),
        out_shape=jax.ShapeDtypeStruct((G, n_exp, cap, W), h32.dtype),
        compiler_params=_params(2), name="moe_gather",
    )(idx_flat, h32)


def _ffn_kernel(x_ref, wg_ref, wu_ref, wd_ref, gt_ref, o_ref):
    x = x_ref[...]
    a = jnp.dot(x, wg_ref[...], preferred_element_type=F32)
    u = jnp.dot(x, wu_ref[...], preferred_element_type=F32)
    h = (a * jax.nn.sigmoid(a) * u).astype(BF16)
    o_ref[...] = jnp.dot(h, wd_ref[...], preferred_element_type=F32) * gt_ref[...]


def _expert_ffn(xs, gates, wg, wu, wd):
    G, E, C, D = xs.shape
    FF = wg.shape[-1]
    slot = lambda e, g: (g, e, 0, 0)
    return pl.pallas_call(
        _ffn_kernel, grid=(E, G),
        in_specs=[pl.BlockSpec((None, None, C, D), slot),
                  pl.BlockSpec((None, D, FF), lambda e, g: (e, 0, 0)),
                  pl.BlockSpec((None, D, FF), lambda e, g: (e, 0, 0)),
                  pl.BlockSpec((None, FF, D), lambda e, g: (e, 0, 0)),
                  pl.BlockSpec((None, None, C, 1), slot)],
        out_specs=pl.BlockSpec((None, None, C, D), slot),
        out_shape=jax.ShapeDtypeStruct((G, E, C, D), F32),
        compiler_params=_params(2), name="moe_expert_ffn",
    )(xs, wg, wu, wd, gates)


def _combine_kernel(idx_ref, y_ref, x_ref, gate_ref, o_ref, *, cap, n_exp):
    e = pl.program_id(2)
    base = (pl.program_id(0) * n_exp + e) * cap

    @pl.when(e == 0)
    def _():
        o_ref[...] = jnp.zeros_like(o_ref)

    def body(p, carry):
        r = idx_ref[base + p]
        o_ref[pl.ds(r, 1), :] += y_ref[pl.ds(p, 1), :]
        return carry

    lax.fori_loop(0, cap, body, 0, unroll=8)

    @pl.when(e == n_exp - 1)
    def _():
        o_ref[...] = x_ref[...] + gate_ref[...] * o_ref[...]


def _combine(idx_flat, y, x, gate):
    G, E, C, D = y.shape
    N = x.shape[1]
    tc = 512
    return pl.pallas_call(
        functools.partial(_combine_kernel, cap=C, n_exp=E),
        grid_spec=pltpu.PrefetchScalarGridSpec(
            num_scalar_prefetch=1, grid=(G, D // tc, E),
            in_specs=[pl.BlockSpec((None, None, C, tc), lambda g, j, e, idx: (g, e, 0, j)),
                      pl.BlockSpec((None, N, tc), lambda g, j, e, idx: (g, 0, j)),
                      pl.BlockSpec((None, 1, tc), lambda g, j, e, idx: (g, 0, j))],
            out_specs=pl.BlockSpec((None, N, tc), lambda g, j, e, idx: (g, 0, j))),
        out_shape=jax.ShapeDtypeStruct((G, N, D), F32),
        compiler_params=_params(3), name="moe_combine",
    )(idx_flat, y, x, gate)


def _moe(x, g2, shift, scale, gate, router_wt, wg, wu, wd):
    G, N, D = x.shape
    E = router_wt.shape[0]
    cap = max(1, (EC_CAPACITY * N) // E)
    h, logits_t = _normmod(x, g2, shift, scale, router_wt)
    idx, gates = _route(logits_t, cap)
    idx_flat = idx.reshape(G * E * cap)
    h32 = lax.bitcast_convert_type(h.reshape(G, N, D // 2, 2), U32)
    xs = lax.bitcast_convert_type(_gather(idx_flat, h32, E, cap), BF16).reshape(G, E, cap, D)
    y = _expert_ffn(xs, gates, wg, wu, wd)
    return _combine(idx_flat, y, x, gate)


def _even_mixer(h_l, h_c, w_in, w_out, q_g, k_g, bias, short_w, short_b, filt_l, filt_c, hy_d,
                mats_l, mats_c, need_ctx):
    scale = NA_HEAD_DIM ** -0.5
    proj_l = _mm(h_l, w_in, tm=1024, tn=1024, out_dtype=BF16, name="even_in_proj")
    proj_c = _mm(h_c, w_in, tm=1024, tn=1024, out_dtype=BF16, name="even_in_proj")
    q_l = _headnorm(proj_l, 0, q_g, scale)
    k_l = _headnorm(proj_l, 1, k_g, 1.0)
    k_c = _headnorm(proj_c, 1, k_g, 1.0)
    na_l = _na_attention(q_l, k_l, proj_l, k_c, proj_c, bias)
    hy_l = _hyena(proj_l, short_w, short_b, *filt_l, hy_d, mats_l)
    y_l = jnp.concatenate([na_l, hy_l], axis=-1)
    y_c = None
    if need_ctx:
        q_c = _headnorm(proj_c, 0, q_g, scale)
        na_c = _attention(q_c, [(k_c, proj_c)], n_heads=NA_HEADS, dq=NA_HEAD_DIM, dv=NA_HEAD_DIM,
                          v_block=lambda h: 2 * NA_HEADS + h, tq=256, name="ctx_attention")
        hy_c = _hyena(proj_c, short_w, short_b, *filt_c, hy_d, mats_c)
        y_c = jnp.concatenate([na_c, hy_c], axis=-1)
    return y_l, y_c, w_out


def _rms_rows(acc, g_ref, width):
    x = acc[:, :width]
    y = x * lax.rsqrt(jnp.mean(x * x, axis=-1, keepdims=True) + EPS) * g_ref[...]
    return y if width == acc.shape[1] else jnp.concatenate([y, acc[:, width:]], axis=1)


def _mla_project(h, w_dq, w_dkv, qa_g, kva_g, w_uq, w_ukv, q_g, k_g, cos, sin, with_q):
    scale = MLA_QK ** -0.5
    kvw = w_dkv.shape[1]
    dkv = _mm(h, w_dkv, tm=1024, tn=kvw, out_dtype=BF16, name="mla_down_kv",
              epi=lambda acc, g_ref: _rms_rows(acc, g_ref, MLA_KV_LORA),
              extras=[(kva_g.reshape(1, -1), (1, MLA_KV_LORA), lambda g, i, j: (0, 0))])
    kv = _mm(dkv, w_ukv, tm=1024, tn=1024, out_dtype=BF16, name="mla_up_kv")
    k = _mla_prep(kv, dkv, k_g, cos, sin)
    q = None
    if with_q:
        qa = _mm(h, w_dq, tm=1024, tn=MLA_Q_LORA, out_dtype=BF16, name="mla_down_q",
                 epi=lambda acc, g_ref: _rms_rows(acc, g_ref, MLA_Q_LORA),
                 extras=[(qa_g.reshape(1, -1), (1, MLA_Q_LORA), lambda g, i, j: (0, 0))])
        q_raw = _mm(qa, w_uq, tm=1024, tn=1024, out_dtype=BF16, name="mla_up_q")
        q = _mla_prep(q_raw, None, q_g, cos, sin, scale)
    return q, k, kv


def _pad_heads(w, head_w):
    k = w.shape[0]
    w = w.reshape(k, MLA_HEADS, head_w)
    return jnp.pad(w, ((0, 0), (0, 0), (0, MLA_HEAD_PAD - head_w))).reshape(k, MLA_HEADS * MLA_HEAD_PAD)


def _mla_mixer(h_l, h_c, w_down, qa_g, kva_g, w_uq, w_ukv, q_g, k_g, w_o, rope_l, rope_c, need_ctx):
    w_dq = w_down[:, :MLA_Q_LORA].astype(BF16)
    w_dkv = jnp.pad(w_down[:, MLA_Q_LORA:], ((0, 0), (0, LANES - MLA_ROPE))).astype(BF16)
    w_uq_p = _pad_heads(w_uq, MLA_QK).astype(BF16)
    w_ukv_b = w_ukv.astype(BF16)
    qg = jnp.pad(q_g, (0, MLA_HEAD_PAD - MLA_QK)).reshape(1, MLA_HEAD_PAD)
    kg = jnp.pad(k_g, (0, MLA_HEAD_PAD - MLA_QK)).reshape(1, MLA_HEAD_PAD)
    proj = functools.partial(_mla_project, w_dq=w_dq, w_dkv=w_dkv, qa_g=qa_g, kva_g=kva_g, w_uq=w_uq_p,
                             w_ukv=w_ukv_b, q_g=qg, k_g=kg)
    q_l, k_l, kv_l = proj(h_l, cos=rope_l[0], sin=rope_l[1], with_q=True)
    q_c, k_c, kv_c = proj(h_c, cos=rope_c[0], sin=rope_c[1], with_q=need_ctx)
    attn = functools.partial(_attention, n_heads=MLA_HEADS, dq=MLA_HEAD_PAD, dv=MLA_V,
                             v_block=lambda h: 2 * h + 1)
    y_l = attn(q_l, [(k_c, kv_c), (k_l, kv_l)], tq=512, name="mla_attention")
    y_c = attn(q_c, [(k_c, kv_c)], tq=256, name="ctx_attention") if need_ctx else None
    return y_l, y_c, w_o.astype(BF16)


def kernel(x, c, ctx, c_ctx, ada_w, ada_b, norm1_g, norm2_g, router_w, moe_w_gate, moe_w_up, moe_w_down, ev_w_in, ev_w_out, na_q_g, na_k_g, na_rpb, hy_short_w, hy_short_b, hy_w1, hy_b1, hy_w2, hy_b2, hy_w3, hy_freq, hy_d, mla_w_down, mla_qa_g, mla_kva_g, mla_w_uq, mla_w_ukv, mla_q_g, mla_k_g, mla_w_o):
    B, S, D = x.shape
    lc = ctx.shape[1]
    depth = ada_w.shape[0]
    assert B + 1 <= 8
    c8 = jnp.concatenate([c, c_ctx[None, :], jnp.zeros((8 - B - 1, D), F32)], axis=0)
    mods = _ada_modulation(c8, ada_w, ada_b)
    mats_l, mats_c = _dft_mats(S), _dft_mats(lc)
    rope_l = _rope_tables(S)
    rope_c = (jnp.ones((lc, LANES), F32), jnp.zeros((lc, LANES), F32))
    for i in range(depth):
        last = i == depth - 1
        j = i // 2
        m_l = [mods[i, :B, None, k * D:(k + 1) * D] for k in range(N_MOD)]
        m_c = [jnp.broadcast_to(mods[i, B, None, None, k * D:(k + 1) * D], (B, 1, D)) for k in range(N_MOD)]
        h_l = _normmod(x, norm1_g[i], m_l[0], m_l[1])
        h_c = _normmod(ctx, norm1_g[i], m_c[0], m_c[1])
        if i % 2 == 0:
            filt = functools.partial(_hyena_filters, w1=hy_w1[j], b1=hy_b1[j], w2=hy_w2[j], b2=hy_b2[j],
                                     w3=hy_w3[j], freq=hy_freq[j])
            y_l, y_c, w_o = _even_mixer(
                h_l, h_c, ev_w_in[j].astype(BF16), ev_w_out[j].astype(BF16), na_q_g[j], na_k_g[j],
                _na_bias(na_rpb[j]), hy_short_w[j], hy_short_b[j], filt(S), filt(lc) if not last else None,
                hy_d[j], mats_l, mats_c, not last)
        else:
            y_l, y_c, w_o = _mla_mixer(h_l, h_c, mla_w_down[j], mla_qa_g[j], mla_kva_g[j], mla_w_uq[j],
                                       mla_w_ukv[j], mla_q_g[j], mla_k_g[j], mla_w_o[j], rope_l, rope_c,
                                       not last)
        rwt = router_w[i].T
        wg, wu, wd = (w[i].astype(BF16) for w in (moe_w_gate, moe_w_up, moe_w_down))
        x = _gated_residual_mm(y_l, w_o, x, m_l[2], "mixer_out_proj")
        x = _moe(x, norm2_g[i], m_l[3], m_l[4], m_l[5], rwt, wg, wu, wd)
        if not last:
            ctx = _gated_residual_mm(y_c, w_o, ctx, m_c[2], "mixer_out_proj")
            ctx = _moe(ctx, norm2_g[i], m_c[3], m_c[4], m_c[5], rwt, wg, wu, wd)
    return x
```

```python
import functools
import math

import numpy as np
import jax
import jax.numpy as jnp
from jax import lax
from jax.experimental import pallas as pl
from jax.experimental.pallas import tpu as pltpu

F32, BF16, I32, U32 = jnp.float32, jnp.bfloat16, jnp.int32, jnp.uint32
HIGHEST = lax.Precision.HIGHEST

D_MODEL = 2048
DEPTH = 4
GRID_W = 64
N_MOD = 6
NA_HEADS = 8
NA_HEAD_DIM = 128
NA_WIDTH = NA_HEADS * NA_HEAD_DIM
NA_WIN_ROWS = 8
NA_WIN_COLS = 16
HY_WIDTH = D_MODEL - NA_WIDTH
HY_ORDER = 2
HY_EMB = 33
HY_BANDS = (HY_EMB - 1) // 2
HY_HIDDEN = 64
HY_DECAY_SHORT = 0.3
HY_DECAY_LONG = 1.5
HY_DECAY_TARGET = 1e-2
MLA_HEADS = 16
MLA_Q_LORA = 768
MLA_KV_LORA = 512
MLA_NOPE = 128
MLA_ROPE = 64
MLA_V = 128
MLA_QK = MLA_NOPE + MLA_ROPE
MLA_HEAD_PAD = 256
ROPE_BASE = 10000.0
N_EXPERTS = 16
EC_CAPACITY = 2
EXPERT_FF = 1024
EPS = 1e-6
NEG_INF = -1e30

LANES = 128
V7X_VMEM_LIMIT_BYTES = 56 * 1024 * 1024


def _params(n_axes):
    return pltpu.CompilerParams(dimension_semantics=("arbitrary",) * n_axes,
                                vmem_limit_bytes=V7X_VMEM_LIMIT_BYTES)


def _nt(a, b):
    return lax.dot_general(a, b, (((1,), (1,)), ((), ())), preferred_element_type=F32)


def _mm(a, b, *, tm, tn, out_dtype, epi=None, extras=(), order="gij", precision=None, name="mm"):
    a3, b3 = a.ndim == 3, b.ndim == 3
    G = a.shape[0] if a3 else (b.shape[0] if b3 else 1)
    M = a.shape[-2]
    K, N = b.shape[-2:]
    tm, tn = min(tm, M), min(tn, N)
    assert M % tm == 0 and N % tn == 0 and a.shape[-1] >= K
    sizes = dict(g=G, i=M // tm, j=N // tn)
    grid = tuple(sizes[c] for c in order)

    def gij(pid):
        d = dict(zip(order, pid))
        return d["g"], d["i"], d["j"]

    def a_map(*pid):
        g, i, _ = gij(pid)
        return (g, i, 0) if a3 else (i, 0)

    def b_map(*pid):
        g, _, j = gij(pid)
        return (g, 0, j) if b3 else (0, j)

    in_specs = [pl.BlockSpec((None, tm, K) if a3 else (tm, K), a_map),
                pl.BlockSpec((None, K, tn) if b3 else (K, tn), b_map)]
    for _, blk, fn in extras:
        in_specs.append(pl.BlockSpec(blk, lambda *pid, fn=fn: fn(*gij(pid))))

    def body(a_ref, b_ref, *rest):
        o_ref = rest[-1]
        acc = jnp.dot(a_ref[...], b_ref[...], preferred_element_type=F32, precision=precision)
        if epi is not None:
            acc = epi(acc, *rest[:-1])
        o_ref[...] = acc.astype(o_ref.dtype)

    return pl.pallas_call(
        body, grid=grid, in_specs=in_specs,
        out_specs=pl.BlockSpec((None, tm, tn), lambda *pid: gij(pid)),
        out_shape=jax.ShapeDtypeStruct((G, M, N), out_dtype),
        compiler_params=_params(3), name=name,
    )(a, b, *[e[0] for e in extras])


def _gated_residual_mm(a, w, res, gate, name):
    tm, tn = min(1024, a.shape[1]), min(1024, w.shape[1])
    return _mm(a, w, tm=tm, tn=tn, out_dtype=F32, name=name,
               epi=lambda acc, r_ref, g_ref: r_ref[...] + g_ref[...] * acc,
               extras=[(res, (None, tm, tn), lambda g, i, j: (g, i, j)),
                       (gate, (None, 1, tn), lambda g, i, j: (g, 0, j))])


def _ada_kernel(c_ref, w_ref, b_ref, o_ref):
    c = c_ref[...]
    sc = c * jax.nn.sigmoid(c)
    o_ref[...] = jnp.dot(sc, w_ref[...], preferred_element_type=F32, precision=HIGHEST) + b_ref[...]


def _ada_modulation(c8, ada_w, ada_b):
    depth, d, n = ada_w.shape
    tn = 1024
    return pl.pallas_call(
        _ada_kernel, grid=(depth, n // tn),
        in_specs=[pl.BlockSpec((8, d), lambda l, j: (0, 0)),
                  pl.BlockSpec((None, d, tn), lambda l, j: (l, 0, j)),
                  pl.BlockSpec((None, 1, tn), lambda l, j: (l, 0, j))],
        out_specs=pl.BlockSpec((None, 8, tn), lambda l, j: (l, 0, j)),
        out_shape=jax.ShapeDtypeStruct((depth, 8, n), F32),
        compiler_params=_params(2), name="ada_modulation",
    )(c8, ada_w, ada_b.reshape(depth, 1, n))


def _normmod_kernel(x_ref, g_ref, sh_ref, sc_ref, *rest, with_router):
    x = x_ref[...]
    y = x * lax.rsqrt(jnp.mean(x * x, axis=-1, keepdims=True) + EPS) * g_ref[...]
    h = y * (1.0 + sc_ref[...]) + sh_ref[...]
    if with_router:
        rw_ref, o_ref, lg_ref = rest
        lg_ref[...] = lax.dot_general(rw_ref[...], h, (((1,), (1,)), ((), ())),
                                      preferred_element_type=F32, precision=HIGHEST)
        o_ref[...] = _pack_bf16_pairs(h)
    else:
        (o_ref,) = rest
        o_ref[...] = h.astype(o_ref.dtype)


def _pack_bf16_pairs(h):
    half = h.shape[1] // 2
    bits = pltpu.bitcast(h.astype(BF16).astype(F32), U32)
    return (bits[:, :half] >> 16) | (bits[:, half:] & jnp.uint32(0xFFFF0000))


def _unpack_bf16_pairs(p):
    lo = pltpu.bitcast(p << 16, F32)
    hi = pltpu.bitcast(p & jnp.uint32(0xFFFF0000), F32)
    return jnp.concatenate([lo, hi], axis=1).astype(BF16)


def _normmod(x, g, shift, scale, router_wt=None):
    G, M, D = x.shape
    tm = min(512, M)
    with_router = router_wt is not None
    in_specs = [pl.BlockSpec((None, tm, D), lambda g_, i: (g_, i, 0)),
                pl.BlockSpec((1, D), lambda g_, i: (0, 0)),
                pl.BlockSpec((None, 1, D), lambda g_, i: (g_, 0, 0)),
                pl.BlockSpec((None, 1, D), lambda g_, i: (g_, 0, 0))]
    out_specs = [pl.BlockSpec((None, tm, D), lambda g_, i: (g_, i, 0))]
    out_shape = [jax.ShapeDtypeStruct((G, M, D), BF16)]
    args = [x, g.reshape(1, D), shift, scale]
    if with_router:
        E = router_wt.shape[0]
        out_specs = [pl.BlockSpec((None, tm, D // 2), lambda g_, i: (g_, i, 0))]
        out_shape = [jax.ShapeDtypeStruct((G, M, D // 2), U32)]
        in_specs.append(pl.BlockSpec((E, D), lambda g_, i: (0, 0)))
        out_specs.append(pl.BlockSpec((None, E, tm), lambda g_, i: (g_, 0, i)))
        out_shape.append(jax.ShapeDtypeStruct((G, E, M), F32))
        args.append(router_wt)
    out = pl.pallas_call(
        functools.partial(_normmod_kernel, with_router=with_router), grid=(G, M // tm),
        in_specs=in_specs, out_specs=out_specs, out_shape=out_shape,
        compiler_params=_params(2), name="normmod_router" if with_router else "normmod",
    )(*args)
    return out if with_router else out[0]


def _headnorm_kernel(x_ref, g_ref, o_ref, *, n_heads, head_dim, scale):
    for h in range(n_heads):
        sl = slice(h * head_dim, (h + 1) * head_dim)
        x = x_ref[:, sl].astype(F32)
        y = x * lax.rsqrt(jnp.mean(x * x, axis=-1, keepdims=True) + EPS) * g_ref[...]
        o_ref[:, sl] = (y * scale).astype(o_ref.dtype)


def _headnorm(x, lane_block, g, scale):
    G, M, _ = x.shape
    tm = min(512, M)
    return pl.pallas_call(
        functools.partial(_headnorm_kernel, n_heads=NA_HEADS, head_dim=NA_HEAD_DIM, scale=scale),
        grid=(G, M // tm),
        in_specs=[pl.BlockSpec((None, tm, NA_WIDTH), lambda g_, i: (g_, i, lane_block)),
                  pl.BlockSpec((1, NA_HEAD_DIM), lambda g_, i: (0, 0))],
        out_specs=pl.BlockSpec((None, tm, NA_WIDTH), lambda g_, i: (g_, i, 0)),
        out_shape=jax.ShapeDtypeStruct((G, M, NA_WIDTH), BF16),
        compiler_params=_params(2), name="headnorm",
    )(x, g.reshape(1, NA_HEAD_DIM))


def _attn_kernel(q_ref, k_ref, v_ref, o_ref):
    s = _nt(q_ref[...], k_ref[...])
    p = jnp.exp(s - jnp.max(s, axis=-1, keepdims=True))
    l = jnp.sum(p, axis=-1, keepdims=True)
    o = jnp.dot(p.astype(BF16), v_ref[...], preferred_element_type=F32)
    o_ref[...] = (o / l).astype(o_ref.dtype)


def _attention(q, k, v, *, n_heads, dq, dv, v_block):
    G, L, _ = q.shape
    return pl.pallas_call(
        _attn_kernel, grid=(G, n_heads),
        in_specs=[pl.BlockSpec((None, L, dq), lambda g, h: (g, 0, h)),
                  pl.BlockSpec((None, L, dq), lambda g, h: (g, 0, h)),
                  pl.BlockSpec((None, L, dv), lambda g, h: (g, 0, v_block(h)))],
        out_specs=pl.BlockSpec((None, L, dv), lambda g, h: (g, 0, h)),
        out_shape=jax.ShapeDtypeStruct((G, L, n_heads * dv), BF16),
        compiler_params=_params(2), name="ctx_attention",
    )(q, k, v)


MLA_KEY_BLOCK = 512


def _mla_attn_kernel(q_ref, kc_ref, vc_ref, kl_ref, vl_ref, o_ref):
    q = q_ref[...]
    s = _nt(q, kc_ref[...])
    m = jnp.max(s, axis=-1, keepdims=True)
    p = jnp.exp(s - m)
    l = jnp.sum(p, axis=-1, keepdims=True)
    acc = jnp.dot(p.astype(BF16), vc_ref[...], preferred_element_type=F32)
    for j in range(kl_ref.shape[0] // MLA_KEY_BLOCK):
        rows = slice(j * MLA_KEY_BLOCK, (j + 1) * MLA_KEY_BLOCK)
        s = _nt(q, kl_ref[rows, :])
        m_new = jnp.maximum(m, jnp.max(s, axis=-1, keepdims=True))
        alpha = jnp.exp(m - m_new)
        p = jnp.exp(s - m_new)
        l = alpha * l + jnp.sum(p, axis=-1, keepdims=True)
        acc = alpha * acc + jnp.dot(p.astype(BF16), vl_ref[rows, :], preferred_element_type=F32)
        m = m_new
    o_ref[...] = (acc / l).astype(o_ref.dtype)


def _mla_attention(q, k_c, kv_c, k_l, kv_l):
    G, S, _ = q.shape
    lc = k_c.shape[1]
    tq = min(512, S)
    assert S % MLA_KEY_BLOCK == 0
    dq, dv = MLA_HEAD_PAD, MLA_V
    return pl.pallas_call(
        _mla_attn_kernel, grid=(G, MLA_HEADS, S // tq),
        in_specs=[pl.BlockSpec((None, tq, dq), lambda g, h, i: (g, i, h)),
                  pl.BlockSpec((None, lc, dq), lambda g, h, i: (g, 0, h)),
                  pl.BlockSpec((None, lc, dv), lambda g, h, i: (g, 0, 2 * h + 1)),
                  pl.BlockSpec((None, S, dq), lambda g, h, i: (g, 0, h)),
                  pl.BlockSpec((None, S, dv), lambda g, h, i: (g, 0, 2 * h + 1))],
        out_specs=pl.BlockSpec((None, tq, dv), lambda g, h, i: (g, i, h)),
        out_shape=jax.ShapeDtypeStruct((G, S, MLA_HEADS * dv), BF16),
        compiler_params=_params(3), name="mla_attention",
    )(q, k_c, kv_c, k_l, kv_l)


NA_ROWS_PER_STEP = 8


def _na_kernel(q_ref, k_ref, v_ref, kc_ref, vc_ref, bias_ref, o_ref, *, n_rows):
    rb = pl.program_id(2)
    kc = kc_ref[...]
    vc = vc_ref[...]
    win = NA_WIN_ROWS * GRID_W
    for i in range(NA_ROWS_PER_STEP):
        r = rb * NA_ROWS_PER_STEP + i
        r0 = jnp.clip(r - NA_WIN_ROWS // 2, 0, n_rows - NA_WIN_ROWS)
        start = pl.multiple_of(r0 * GRID_W, GRID_W)
        q = q_ref[i * GRID_W:(i + 1) * GRID_W, :]
        s_w = _nt(q, k_ref[pl.ds(start, win), :]) + bias_ref[r - r0]
        s_c = _nt(q, kc)
        m = jnp.maximum(jnp.max(s_w, axis=-1, keepdims=True), jnp.max(s_c, axis=-1, keepdims=True))
        p_w = jnp.exp(s_w - m)
        p_c = jnp.exp(s_c - m)
        l = jnp.sum(p_w, axis=-1, keepdims=True) + jnp.sum(p_c, axis=-1, keepdims=True)
        o = (jnp.dot(p_w.astype(BF16), v_ref[pl.ds(start, win), :], preferred_element_type=F32)
             + jnp.dot(p_c.astype(BF16), vc, preferred_element_type=F32))
        o_ref[i * GRID_W:(i + 1) * GRID_W, :] = (o / l).astype(o_ref.dtype)


def _na_attention(q, k, proj, kc, proj_c, bias):
    G, S, _ = q.shape
    lc = kc.shape[1]
    n_rows = S // GRID_W
    assert n_rows >= NA_WIN_ROWS and n_rows % NA_ROWS_PER_STEP == 0
    tq = NA_ROWS_PER_STEP * GRID_W
    hd = NA_HEAD_DIM
    v_off = 2 * NA_HEADS
    return pl.pallas_call(
        functools.partial(_na_kernel, n_rows=n_rows),
        grid=(G, NA_HEADS, n_rows // NA_ROWS_PER_STEP),
        in_specs=[pl.BlockSpec((None, tq, hd), lambda g, h, r: (g, r, h)),
                  pl.BlockSpec((None, S, hd), lambda g, h, r: (g, 0, h)),
                  pl.BlockSpec((None, S, hd), lambda g, h, r: (g, 0, v_off + h)),
                  pl.BlockSpec((None, lc, hd), lambda g, h, r: (g, 0, h)),
                  pl.BlockSpec((None, lc, hd), lambda g, h, r: (g, 0, v_off + h)),
                  pl.BlockSpec((None, NA_WIN_ROWS, GRID_W, NA_WIN_ROWS * GRID_W),
                               lambda g, h, r: (h, 0, 0, 0))],
        out_specs=pl.BlockSpec((None, tq, hd), lambda g, h, r: (g, r, h)),
        out_shape=jax.ShapeDtypeStruct((G, S, NA_WIDTH), BF16),
        compiler_params=_params(3), name="na_attention",
    )(q, k, proj, kc, proj_c, bias)


def _na_bias(rpb):
    cols = np.arange(GRID_W)
    col_start = np.clip(cols - NA_WIN_COLS // 2, 0, GRID_W - NA_WIN_COLS)
    in_win = (cols[None, :] >= col_start[:, None]) & (cols[None, :] < col_start[:, None] + NA_WIN_COLS)
    wr, wc = NA_WIN_ROWS, NA_WIN_COLS
    a = jnp.stack([rpb[:, wr - 1 - v:2 * wr - 1 - v, :] for v in range(wr)], axis=1).astype(F32)
    pad = GRID_W - wc
    ap = jnp.pad(a, ((0, 0), (0, 0), (0, 0), (pad, pad)))
    b = jnp.stack([ap[..., GRID_W - 1 - q:2 * GRID_W - 1 - q] for q in range(GRID_W)], axis=3)
    b = jnp.transpose(b, (0, 1, 3, 2, 4))
    b = jnp.where(in_win[None, None, :, None, :], b, NEG_INF)
    return b.reshape(rpb.shape[0], wr, GRID_W, wr * GRID_W)


def _shortconv_kernel(x_ref, w_ref, b_ref, o_ref):
    x = x_ref[...].astype(F32)
    n = x.shape[0]
    row = lax.broadcasted_iota(I32, x.shape, 0)
    prev = jnp.where(row == 0, 0.0, pltpu.roll(x, 1, 0))
    nxt = jnp.where(row == n - 1, 0.0, pltpu.roll(x, n - 1, 0))
    w = w_ref[...]
    o_ref[...] = (w[0:1] * prev + w[1:2] * x + w[2:3] * nxt + b_ref[...]).astype(o_ref.dtype)


def _shortconv(proj, lane_off, w, b):
    G, L, _ = proj.shape
    C = w.shape[1]
    tc = 256
    off = lane_off // tc
    return pl.pallas_call(
        _shortconv_kernel, grid=(G, C // tc),
        in_specs=[pl.BlockSpec((None, L, tc), lambda g, j: (g, 0, off + j)),
                  pl.BlockSpec((3, tc), lambda g, j: (0, j)),
                  pl.BlockSpec((1, tc), lambda g, j: (0, j))],
        out_specs=pl.BlockSpec((None, L, tc), lambda g, j: (g, 0, j)),
        out_shape=jax.ShapeDtypeStruct((G, L, C), BF16),
        compiler_params=_params(2), name="hyena_shortconv",
    )(proj, w, b.reshape(1, C))


def _hyfilt_kernel(z_ref, w1_ref, b1_ref, w2_ref, b2_ref, fr_ref, w3f_ref, w3b_ref, dl_ref, os_ref, od_ref):
    z = z_ref[...]
    fr = fr_ref[...]
    hid = jnp.sin(fr * (jnp.dot(z, w1_ref[...], preferred_element_type=F32, precision=HIGHEST) + b1_ref[...]))
    hid = jnp.sin(fr * (jnp.dot(hid, w2_ref[...], preferred_element_type=F32, precision=HIGHEST) + b2_ref[...]))
    decay = jnp.exp(-z[:, 0:1] * dl_ref[...])

    def filt(w3_ref):
        f = jnp.dot(hid, w3_ref[...], preferred_element_type=F32, precision=HIGHEST) * decay
        return f * lax.rsqrt(jnp.sum(f * f, axis=0, keepdims=True) + EPS)

    ff, fb = filt(w3f_ref), filt(w3b_ref)
    os_ref[...] = (ff + fb).astype(os_ref.dtype)
    od_ref[...] = (ff - fb).astype(od_ref.dtype)


def _hyena_filters(L, w1, b1, w2, b2, w3, freq):
    t = jnp.linspace(0.0, 1.0, L, dtype=F32)[:, None]
    wv = 2.0 * math.pi * jnp.arange(L, dtype=F32)[:, None] / L
    bands = jnp.linspace(1e-4, HY_BANDS - 1, HY_BANDS, dtype=F32)[None, :]
    z = jnp.concatenate([t, jnp.cos(bands * wv), -jnp.sin(bands * wv)], axis=-1)
    z = jnp.pad(z, ((0, 0), (0, LANES - HY_EMB)))
    hp = LANES - HY_HIDDEN
    w1p = jnp.pad(w1, ((0, LANES - HY_EMB), (0, hp)))
    w2p = jnp.pad(w2, ((0, hp), (0, hp)))
    w3p = jnp.pad(w3, ((0, hp), (0, 0)))
    b1p = jnp.pad(b1, (0, hp)).reshape(1, LANES)
    b2p = jnp.pad(b2, (0, hp)).reshape(1, LANES)
    frp = jnp.pad(freq, (0, hp)).reshape(1, LANES)
    d_lo = math.log(HY_DECAY_TARGET) / HY_DECAY_LONG
    d_hi = math.log(HY_DECAY_TARGET) / HY_DECAY_SHORT
    deltas = jnp.abs(jnp.linspace(d_lo, d_hi, HY_WIDTH, dtype=F32)).reshape(1, HY_WIDTH)
    tc = 256
    nct = HY_WIDTH // tc
    const = lambda o, j: (0, 0)
    out = jax.ShapeDtypeStruct((L, HY_ORDER * HY_WIDTH), BF16)
    return pl.pallas_call(
        _hyfilt_kernel, grid=(HY_ORDER, nct),
        in_specs=[pl.BlockSpec((L, LANES), const), pl.BlockSpec((LANES, LANES), const),
                  pl.BlockSpec((1, LANES), const), pl.BlockSpec((LANES, LANES), const),
                  pl.BlockSpec((1, LANES), const), pl.BlockSpec((1, LANES), const),
                  pl.BlockSpec((LANES, tc), lambda o, j: (0, 2 * o * nct + j)),
                  pl.BlockSpec((LANES, tc), lambda o, j: (0, (2 * o + 1) * nct + j)),
                  pl.BlockSpec((1, tc), lambda o, j: (0, j))],
        out_specs=[pl.BlockSpec((L, tc), lambda o, j: (0, o * nct + j))] * 2,
        out_shape=[out, out], compiler_params=_params(2), name="hyena_filters",
    )(z, w1p, b1p, w2p, b2p, frp, w3p, w3p, deltas)


def _dft_tables(L):
    A = 1 << (int(math.log2(L)) // 2)
    period = 4 * L
    k = np.arange(L, dtype=np.int64)[:, None]
    ang_a = 2.0 * np.pi * (((2 * k + 1) * (A * np.arange(L // A, dtype=np.int64)[None, :])) % period) / period
    ang_b = 2.0 * np.pi * (((2 * k + 1) * np.arange(A, dtype=np.int64)[None, :]) % period) / period
    return tuple(np.asarray(f(x), np.float32) for x in (ang_a, ang_b) for f in (np.cos, np.sin))


def _dft_mats(L):
    ca, sa, cb, sb = (jnp.asarray(t) for t in _dft_tables(L))
    fc = (ca[:, :, None] * cb[:, None, :] - sa[:, :, None] * sb[:, None, :]).reshape(L, L)
    fs = (sa[:, :, None] * cb[:, None, :] + ca[:, :, None] * sb[:, None, :]).reshape(L, L)
    cat, sat, cbt, sbt = ca.T, sa.T, cb.T, sb.T
    fct = (cat[:, None, :] * cbt[None, :, :] - sat[:, None, :] * sbt[None, :, :]).reshape(L, L)
    fst = (sat[:, None, :] * cbt[None, :, :] + cat[:, None, :] * sbt[None, :, :]).reshape(L, L)
    return tuple(m.astype(BF16) for m in (fc, fs, fct, fst))


def _dft_fwd_kernel(fc_ref, fs_ref, u_ref, hc_ref, hs_ref, yc_ref, ys_ref):
    u = u_ref[...]
    xc = jnp.dot(fc_ref[...], u, preferred_element_type=F32)
    xs = jnp.dot(fs_ref[...], u, preferred_element_type=F32)
    hc, hs = hc_ref[...], hs_ref[...]
    yc_ref[...] = (xc * hc - xs * hs).astype(yc_ref.dtype)
    ys_ref[...] = (xc * hs + xs * hc).astype(ys_ref.dtype)


def _dft_fwd(fc, fs, u, u_block, hc, hs, h_block):
    G, L, _ = u.shape
    C = HY_WIDTH
    tr, tc = min(512, L), 512
    nct = C // tc
    spec_f = pl.BlockSpec((tr, L), lambda k, g, j: (k, 0))
    spec_h = pl.BlockSpec((None, tr, tc), lambda k, g, j: (0, k, h_block * nct + j))
    spec_y = pl.BlockSpec((None, tr, tc), lambda k, g, j: (g, k, j))
    out = jax.ShapeDtypeStruct((G, L, C), BF16)
    return pl.pallas_call(
        _dft_fwd_kernel, grid=(L // tr, G, nct),
        in_specs=[spec_f, spec_f, pl.BlockSpec((None, L, tc), lambda k, g, j: (g, 0, u_block * nct + j)),
                  spec_h, spec_h],
        out_specs=[spec_y, spec_y], out_shape=[out, out],
        compiler_params=_params(3), name="hyena_dft_fwd",
    )(fc, fs, u, hc, hs)


def _dft_inv_kernel(fct_ref, fst_ref, yc_ref, ys_ref, u_ref, x_ref, d_ref, o_ref, *, inv_scale):
    y = (jnp.dot(fct_ref[...], yc_ref[...], preferred_element_type=F32)
         + jnp.dot(fst_ref[...], ys_ref[...], preferred_element_type=F32)) * inv_scale
    y = y + u_ref[...].astype(F32) * d_ref[...]
    o_ref[...] = (x_ref[...].astype(F32) * y).astype(o_ref.dtype)


def _dft_inv(fct, fst, yc, ys, u, u_block, gate, gate_block, d):
    G, L, C = yc.shape
    tr, tc = min(512, L), 512
    nct = C // tc
    spec_f = pl.BlockSpec((tr, L), lambda n, g, j: (n, 0))
    spec_y = pl.BlockSpec((None, L, tc), lambda n, g, j: (g, 0, j))
    return pl.pallas_call(
        functools.partial(_dft_inv_kernel, inv_scale=1.0 / L),
        grid=(L // tr, G, nct),
        in_specs=[spec_f, spec_f, spec_y, spec_y,
                  pl.BlockSpec((None, tr, tc), lambda n, g, j: (g, n, u_block * nct + j)),
                  pl.BlockSpec((None, tr, tc), lambda n, g, j: (g, n, gate_block * nct + j)),
                  pl.BlockSpec((1, tc), lambda n, g, j: (0, j))],
        out_specs=pl.BlockSpec((None, tr, tc), lambda n, g, j: (g, n, j)),
        out_shape=jax.ShapeDtypeStruct((G, L, C), BF16),
        compiler_params=_params(3), name="hyena_dft_inv",
    )(fct, fst, yc, ys, u, gate, d.reshape(1, C))


def _hyena(proj, short_w, short_b, hsum, hdiff, hy_d, mats):
    fc, fs, fct, fst = mats
    sc = _shortconv(proj, 3 * NA_WIDTH, short_w, short_b)
    hc = _mm(fc, hsum, tm=512, tn=512, out_dtype=F32, name="hyena_filter_dft")
    hs = _mm(fs, hdiff, tm=512, tn=512, out_dtype=F32, name="hyena_filter_dft")
    z, z_block = sc, 0
    for o in range(HY_ORDER):
        yc, ys = _dft_fwd(fc, fs, z, z_block, hc, hs, o)
        z = _dft_inv(fct, fst, yc, ys, z, z_block, sc, o + 1, hy_d[o])
        z_block = 0
    return z


def _rope_rotate(y):
    lane = lax.broadcasted_iota(I32, y.shape, 1)
    first = (lane % 32) < 16
    return jnp.where(first, -pltpu.roll(y, LANES - 16, 1), pltpu.roll(y, 16, 1))


def _mla_qprep_kernel(x_ref, g_ref, cos_ref, sin_ref, o_ref, *, scale):
    cos, sin = cos_ref[...], sin_ref[...]
    g = g_ref[...]
    for h in range(MLA_HEADS):
        lo = h * MLA_HEAD_PAD
        xn = x_ref[:, lo:lo + LANES].astype(F32)
        xr = x_ref[:, lo + LANES:lo + 2 * LANES].astype(F32)
        ss = jnp.sum(xn * xn, axis=-1, keepdims=True) + jnp.sum(xr * xr, axis=-1, keepdims=True)
        rs = lax.rsqrt(ss * (1.0 / MLA_QK) + EPS) * scale
        yr = xr * rs * g[:, LANES:]
        o_ref[:, lo:lo + LANES] = (xn * rs * g[:, :LANES]).astype(o_ref.dtype)
        o_ref[:, lo + LANES:lo + 2 * LANES] = (yr * cos + _rope_rotate(yr) * sin).astype(o_ref.dtype)


def _mla_kprep_kernel(kv_ref, kr_ref, g_ref, cos_ref, sin_ref, o_ref):
    cos, sin = cos_ref[...], sin_ref[...]
    g = g_ref[...]
    kr = kr_ref[...].astype(F32)
    sr = jnp.sum(kr * kr, axis=-1, keepdims=True)
    krg = kr * g[:, LANES:]
    krot = krg * cos + _rope_rotate(krg) * sin
    for h in range(MLA_HEADS):
        lo = h * MLA_HEAD_PAD
        kn = kv_ref[:, lo:lo + LANES].astype(F32)
        rs = lax.rsqrt((jnp.sum(kn * kn, axis=-1, keepdims=True) + sr) * (1.0 / MLA_QK) + EPS)
        o_ref[:, lo:lo + LANES] = (kn * rs * g[:, :LANES]).astype(o_ref.dtype)
        o_ref[:, lo + LANES:lo + 2 * LANES] = (krot * rs).astype(o_ref.dtype)


def _mla_prep(x, kr, g_pad, cos, sin, scale=None):
    G, M, W = x.shape
    tm = min(256, M)
    row = lambda g_, i: (g_, i, 0)
    tab = pl.BlockSpec((tm, LANES), lambda g_, i: (i, 0))
    gsp = pl.BlockSpec((1, MLA_HEAD_PAD), lambda g_, i: (0, 0))
    xsp = pl.BlockSpec((None, tm, W), row)
    if kr is None:
        kern, in_specs, args = functools.partial(_mla_qprep_kernel, scale=scale), [xsp, gsp, tab, tab], [x, g_pad, cos, sin]
    else:
        kr_block = MLA_KV_LORA // LANES
        kern = _mla_kprep_kernel
        in_specs = [xsp, pl.BlockSpec((None, tm, LANES), lambda g_, i: (g_, i, kr_block)), gsp, tab, tab]
        args = [x, kr, g_pad, cos, sin]
    return pl.pallas_call(
        kern, grid=(G, M // tm), in_specs=in_specs, out_specs=pl.BlockSpec((None, tm, W), row),
        out_shape=jax.ShapeDtypeStruct((G, M, W), BF16), compiler_params=_params(2),
        name="mla_qprep" if kr is None else "mla_kprep",
    )(*args)


def _rope_tables(S):
    t = jnp.arange(S)
    half = MLA_ROPE // 2
    inv = ROPE_BASE ** (-jnp.arange(0, half, 2, dtype=F32) / half)
    ang_r = (t // GRID_W).astype(F32)[:, None] * inv[None, :]
    ang_c = (t % GRID_W).astype(F32)[:, None] * inv[None, :]
    ang = jnp.concatenate([ang_r, ang_r, ang_c, ang_c], axis=-1)
    pad = LANES - MLA_ROPE
    return (jnp.pad(jnp.cos(ang), ((0, 0), (0, pad)), constant_values=1.0),
            jnp.pad(jnp.sin(ang), ((0, 0), (0, pad))))


def _lane_cumsum(x):
    n = x.shape[1]
    tri = (lax.broadcasted_iota(I32, (LANES, LANES), 0) <= lax.broadcasted_iota(I32, (LANES, LANES), 1))
    tri = jnp.where(tri, 1.0, 0.0).astype(BF16)
    off = jnp.zeros((x.shape[0], 1), F32)
    outs = []
    for j in range(n // LANES):
        c = jnp.dot(x[:, j * LANES:(j + 1) * LANES], tri, preferred_element_type=F32) + off
        outs.append(c)
        off = c[:, LANES - 1:LANES]
    return jnp.concatenate(outs, axis=1)


ROUTE_SLOT_CHUNK = 64


def _route_kernel(lg_ref, idx_ref, gt_ref, aff_s, slot_s, *, cap, chunk):
    lg = lg_ref[...]
    n_exp, n_tok = lg.shape
    ex = jnp.exp(lg - jnp.max(lg, axis=0, keepdims=True))
    aff = ex / jnp.sum(ex, axis=0, keepdims=True)
    bits = pltpu.bitcast(aff, I32)

    def search(i, v):
        cand = v | jnp.left_shift(jnp.int32(1), 30 - i)
        cnt = jnp.sum(jnp.where(bits >= cand, 1.0, 0.0), axis=1, keepdims=True)
        return jnp.where(cnt >= cap, cand, v)

    thr = lax.fori_loop(0, 31, search, jnp.zeros((n_exp, 1), I32))
    gt = bits > thr
    eq = bits == thr
    need = cap - jnp.sum(jnp.where(gt, 1.0, 0.0), axis=1, keepdims=True)
    eq_rank = _lane_cumsum(jnp.where(eq, 1.0, 0.0).astype(BF16))
    sel = jnp.where(gt, 1.0, jnp.where(eq, jnp.where(eq_rank <= need, 1.0, 0.0), 0.0))
    slot = _lane_cumsum(sel.astype(BF16)) * sel
    aff_s[...] = aff
    slot_s[...] = slot
    tok = lax.broadcasted_iota(I32, (chunk, n_tok), 1).astype(F32)

    def per_expert(e, carry):
        srow = slot_s[pl.ds(e, 1), :]
        arow = aff_s[pl.ds(e, 1), :]

        def per_chunk(c, carry2):
            p0 = pl.multiple_of(c * chunk, chunk)
            want = (lax.broadcasted_iota(I32, (chunk, 1), 0) + (p0 + 1)).astype(F32)
            hit = srow == want
            idx_ref[e, pl.ds(p0, chunk), :] = jnp.sum(
                jnp.where(hit, tok, 0.0), axis=1, keepdims=True).astype(I32)
            gt_ref[e, pl.ds(p0, chunk), :] = jnp.sum(
                jnp.where(hit, arow, 0.0), axis=1, keepdims=True)
            return carry2

        return lax.fori_loop(0, cap // chunk, per_chunk, carry)

    lax.fori_loop(0, n_exp, per_expert, 0)


def _route(logits_t, cap):
    G, E, N = logits_t.shape
    chunk = min(ROUTE_SLOT_CHUNK, cap)
    assert cap % chunk == 0 and N % LANES == 0
    kern = functools.partial(_route_kernel, cap=cap, chunk=chunk)
    spec_o = pl.BlockSpec((None, E, cap, 1), lambda g: (g, 0, 0, 0))
    return pl.pallas_call(
        kern, grid=(G,), in_specs=[pl.BlockSpec((None, E, N), lambda g: (g, 0, 0))],
        out_specs=[spec_o, spec_o],
        out_shape=[jax.ShapeDtypeStruct((G, E, cap, 1), I32), jax.ShapeDtypeStruct((G, E, cap, 1), F32)],
        scratch_shapes=[pltpu.VMEM((E, N), F32), pltpu.VMEM((E, N), F32)],
        compiler_params=_params(1), name="moe_route",
    )(logits_t)


def _gather_kernel(idx_ref, h_ref, o_ref, *, cap, n_exp):
    base = (pl.program_id(0) * n_exp + pl.program_id(1)) * cap

    def body(p, carry):
        r = idx_ref[base + p]
        o_ref[pl.ds(p, 1), :] = h_ref[pl.ds(r, 1), :]
        return carry

    lax.fori_loop(0, cap, body, 0, unroll=8)


def _gather(idx_flat, h32, n_exp, cap):
    G, N, W = h32.shape
    return pl.pallas_call(
        functools.partial(_gather_kernel, cap=cap, n_exp=n_exp),
        grid_spec=pltpu.PrefetchScalarGridSpec(
            num_scalar_prefetch=1, grid=(G, n_exp),
            in_specs=[pl.BlockSpec((None, N, W), lambda g, e, idx: (g, 0, 0))],
            out_specs=pl.BlockSpec((None, None, cap, W), lambda g, e, idx: (g, e, 0, 0))),
        out_shape=jax.ShapeDtypeStruct((G, n_exp, cap, W), h32.dtype),
        compiler_params=_params(2), name="moe_gather",
    )(idx_flat, h32)


def _swiglu(x32, slot_gate, feat_gate, wg_ref, wu_ref, wd_ref):
    x = _unpack_bf16_pairs(x32)
    a = jnp.dot(x, wg_ref[...], preferred_element_type=F32)
    u = jnp.dot(x, wu_ref[...], preferred_element_type=F32)
    h = (a * jax.nn.sigmoid(a) * u).astype(BF16)
    return jnp.dot(h, wd_ref[...], preferred_element_type=F32) * slot_gate * feat_gate


def _ffn_kernel(*refs, n_lat, with_ctx):
    if with_ctx:
        xl_ref, gl_ref, fl_ref, xc_ref, gc_ref, fc_ref, wg_ref, wu_ref, wd_ref, yl_ref, yc_ref = refs
    else:
        xl_ref, gl_ref, fl_ref, wg_ref, wu_ref, wd_ref, yl_ref = refs
    step = pl.program_id(1)

    @pl.when(step < n_lat)
    def _():
        yl_ref[...] = _swiglu(xl_ref[...], gl_ref[...], fl_ref[...], wg_ref, wu_ref, wd_ref)

    if with_ctx:
        @pl.when(step == n_lat)
        def _():
            gb, cc, w = xc_ref.shape
            y = _swiglu(xc_ref[...].reshape(gb * cc, w), gc_ref[...].reshape(gb * cc, 1), fc_ref[...],
                        wg_ref, wu_ref, wd_ref)
            yc_ref[...] = y.reshape(gb, cc, y.shape[1])


def _expert_ffn(layer, wg, wu, wd, lat, ctx=None):
    xl, gl, fl = lat
    G, E, C, W = xl.shape
    D, FF = wg.shape[-2:]
    lat_slot = lambda e, s: (jnp.minimum(s, G - 1), e, 0, 0)
    w_idx = lambda e, s: (layer, e, 0, 0)
    in_specs = [pl.BlockSpec((None, None, C, W), lat_slot), pl.BlockSpec((None, None, C, 1), lat_slot),
                pl.BlockSpec((None, 1, D), lambda e, s: (jnp.minimum(s, G - 1), 0, 0))]
    out_specs = [pl.BlockSpec((None, None, C, D), lat_slot)]
    out_shape = [jax.ShapeDtypeStruct((G, E, C, D), F32)]
    args = [xl, gl, fl]
    if ctx is not None:
        xc, gc, fc = ctx
        Gc, _, Cc, _ = xc.shape
        ctx_slot = lambda e, s: (0, e, 0, 0)
        in_specs += [pl.BlockSpec((Gc, None, Cc, W), ctx_slot), pl.BlockSpec((Gc, None, Cc, 1), ctx_slot),
                     pl.BlockSpec((1, D), lambda e, s: (0, 0))]
        out_specs.append(pl.BlockSpec((Gc, None, Cc, D), ctx_slot))
        out_shape.append(jax.ShapeDtypeStruct((Gc, E, Cc, D), F32))
        args += [xc, gc, fc]
    in_specs += [pl.BlockSpec((None, None, D, FF), w_idx), pl.BlockSpec((None, None, D, FF), w_idx),
                 pl.BlockSpec((None, None, FF, D), w_idx)]
    return pl.pallas_call(
        functools.partial(_ffn_kernel, n_lat=G, with_ctx=ctx is not None),
        grid=(E, G + (ctx is not None)), in_specs=in_specs, out_specs=out_specs, out_shape=out_shape,
        compiler_params=_params(2), name="moe_expert_ffn",
    )(*args, wg, wu, wd)


COMBINE_ROWS = 8


def _combine_kernel(idx_ref, y_ref, x_ref, o_ref, *, cap, n_exp):
    e = pl.program_id(2)
    base = (pl.program_id(0) * n_exp + e) * cap

    @pl.when(e == 0)
    def _():
        o_ref[...] = x_ref[...]

    def body(i, carry):
        p0 = pl.multiple_of(i * COMBINE_ROWS, COMBINE_ROWS)
        rows = [idx_ref[base + p0 + k] for k in range(COMBINE_ROWS)]
        vals = [o_ref[pl.ds(rows[k], 1), :] + y_ref[pl.ds(p0 + k, 1), :] for k in range(COMBINE_ROWS)]
        for k in range(COMBINE_ROWS):
            o_ref[pl.ds(rows[k], 1), :] = vals[k]
        return carry

    lax.fori_loop(0, cap // COMBINE_ROWS, body, 0)


def _combine(idx_flat, y, x):
    G, E, C, D = y.shape
    N = x.shape[1]
    tc = 1024
    assert C % COMBINE_ROWS == 0
    return pl.pallas_call(
        functools.partial(_combine_kernel, cap=C, n_exp=E),
        grid_spec=pltpu.PrefetchScalarGridSpec(
            num_scalar_prefetch=1, grid=(G, D // tc, E),
            in_specs=[pl.BlockSpec((None, None, C, tc), lambda g, j, e, idx: (g, e, 0, j)),
                      pl.BlockSpec((None, N, tc), lambda g, j, e, idx: (g, 0, j),
                                   pipeline_mode=pl.Buffered(1))],
            out_specs=pl.BlockSpec((None, N, tc), lambda g, j, e, idx: (g, 0, j))),
        out_shape=jax.ShapeDtypeStruct((G, N, D), F32),
        compiler_params=_params(3), name="moe_combine",
    )(idx_flat, y, x)


def _route_and_gather(x, g2, shift, scale, router_wt):
    G, N, _ = x.shape
    E = router_wt.shape[0]
    cap = max(1, (EC_CAPACITY * N) // E)
    h32, logits_t = _normmod(x, g2, shift, scale, router_wt)
    idx, gates = _route(logits_t, cap)
    idx_flat = idx.reshape(G * E * cap)
    return idx_flat, gates, _gather(idx_flat, h32, E, cap)


def _moe(layer, x, ctx, g2, m_l, m_c, router_wt, wg, wu, wd):
    il, gl, xl = _route_and_gather(x, g2, m_l[3], m_l[4], router_wt)
    if ctx is None:
        (yl,) = _expert_ffn(layer, wg, wu, wd, (xl, gl, m_l[5]))
        return _combine(il, yl, x), None
    ic, gc, xc = _route_and_gather(ctx, g2, m_c[3], m_c[4], router_wt)
    yl, yc = _expert_ffn(layer, wg, wu, wd, (xl, gl, m_l[5]), (xc, gc, m_c[5][0]))
    return _combine(il, yl, x), _combine(ic, yc, ctx)


def _even_mixer(h_l, h_c, w_in, w_out, q_g, k_g, bias, short_w, short_b, filt_l, filt_c, hy_d,
                mats_l, mats_c, need_ctx):
    scale = NA_HEAD_DIM ** -0.5
    proj_l = _mm(h_l, w_in, tm=1024, tn=1024, out_dtype=BF16, name="even_in_proj")
    proj_c = _mm(h_c, w_in, tm=1024, tn=1024, out_dtype=BF16, name="even_in_proj")
    q_l = _headnorm(proj_l, 0, q_g, scale)
    k_l = _headnorm(proj_l, 1, k_g, 1.0)
    k_c = _headnorm(proj_c, 1, k_g, 1.0)
    na_l = _na_attention(q_l, k_l, proj_l, k_c, proj_c, bias)
    hy_l = _hyena(proj_l, short_w, short_b, *filt_l, hy_d, mats_l)
    y_l = jnp.concatenate([na_l, hy_l], axis=-1)
    y_c = None
    if need_ctx:
        q_c = _headnorm(proj_c, 0, q_g, scale)
        na_c = _attention(q_c, k_c, proj_c, n_heads=NA_HEADS, dq=NA_HEAD_DIM, dv=NA_HEAD_DIM,
                          v_block=lambda h: 2 * NA_HEADS + h)
        hy_c = _hyena(proj_c, short_w, short_b, *filt_c, hy_d, mats_c)
        y_c = jnp.concatenate([na_c, hy_c], axis=-1)
    return y_l, y_c, w_out


def _rms_rows(acc, g_ref, width):
    x = acc[:, :width]
    y = x * lax.rsqrt(jnp.mean(x * x, axis=-1, keepdims=True) + EPS) * g_ref[...]
    return y if width == acc.shape[1] else jnp.concatenate([y, acc[:, width:]], axis=1)


def _mla_project(h, w_dq, w_dkv, qa_g, kva_g, w_uq, w_ukv, q_g, k_g, cos, sin, with_q):
    scale = MLA_QK ** -0.5
    kvw = w_dkv.shape[1]
    dkv = _mm(h, w_dkv, tm=1024, tn=kvw, out_dtype=BF16, name="mla_down_kv",
              epi=lambda acc, g_ref: _rms_rows(acc, g_ref, MLA_KV_LORA),
              extras=[(kva_g.reshape(1, -1), (1, MLA_KV_LORA), lambda g, i, j: (0, 0))])
    kv = _mm(dkv, w_ukv, tm=1024, tn=1024, out_dtype=BF16, name="mla_up_kv")
    k = _mla_prep(kv, dkv, k_g, cos, sin)
    q = None
    if with_q:
        qa = _mm(h, w_dq, tm=1024, tn=MLA_Q_LORA, out_dtype=BF16, name="mla_down_q",
                 epi=lambda acc, g_ref: _rms_rows(acc, g_ref, MLA_Q_LORA),
                 extras=[(qa_g.reshape(1, -1), (1, MLA_Q_LORA), lambda g, i, j: (0, 0))])
        q_raw = _mm(qa, w_uq, tm=1024, tn=1024, out_dtype=BF16, name="mla_up_q")
        q = _mla_prep(q_raw, None, q_g, cos, sin, scale)
    return q, k, kv


def _pad_heads(w, head_w):
    k = w.shape[0]
    w = w.reshape(k, MLA_HEADS, head_w)
    return jnp.pad(w, ((0, 0), (0, 0), (0, MLA_HEAD_PAD - head_w))).reshape(k, MLA_HEADS * MLA_HEAD_PAD)


def _mla_mixer(h_l, h_c, w_down, qa_g, kva_g, w_uq, w_ukv, q_g, k_g, w_o, rope_l, rope_c, need_ctx):
    w_dq = w_down[:, :MLA_Q_LORA].astype(BF16)
    w_dkv = jnp.pad(w_down[:, MLA_Q_LORA:], ((0, 0), (0, LANES - MLA_ROPE))).astype(BF16)
    w_uq_p = _pad_heads(w_uq, MLA_QK).astype(BF16)
    w_ukv_b = w_ukv.astype(BF16)
    qg = jnp.pad(q_g, (0, MLA_HEAD_PAD - MLA_QK)).reshape(1, MLA_HEAD_PAD)
    kg = jnp.pad(k_g, (0, MLA_HEAD_PAD - MLA_QK)).reshape(1, MLA_HEAD_PAD)
    proj = functools.partial(_mla_project, w_dq=w_dq, w_dkv=w_dkv, qa_g=qa_g, kva_g=kva_g, w_uq=w_uq_p,
                             w_ukv=w_ukv_b, q_g=qg, k_g=kg)
    q_l, k_l, kv_l = proj(h_l, cos=rope_l[0], sin=rope_l[1], with_q=True)
    q_c, k_c, kv_c = proj(h_c, cos=rope_c[0], sin=rope_c[1], with_q=need_ctx)
    y_l = _mla_attention(q_l, k_c, kv_c, k_l, kv_l)
    y_c = None
    if need_ctx:
        y_c = _attention(q_c, k_c, kv_c, n_heads=MLA_HEADS, dq=MLA_HEAD_PAD, dv=MLA_V,
                         v_block=lambda h: 2 * h + 1)
    return y_l, y_c, w_o.astype(BF16)


def kernel(x, c, ctx, c_ctx, ada_w, ada_b, norm1_g, norm2_g, router_w, moe_w_gate, moe_w_up, moe_w_down, ev_w_in, ev_w_out, na_q_g, na_k_g, na_rpb, hy_short_w, hy_short_b, hy_w1, hy_b1, hy_w2, hy_b2, hy_w3, hy_freq, hy_d, mla_w_down, mla_qa_g, mla_kva_g, mla_w_uq, mla_w_ukv, mla_q_g, mla_k_g, mla_w_o):
    B, S, D = x.shape
    lc = ctx.shape[1]
    depth = ada_w.shape[0]
    assert B + 1 <= 8
    c8 = jnp.concatenate([c, c_ctx[None, :], jnp.zeros((8 - B - 1, D), F32)], axis=0)
    mods = _ada_modulation(c8, ada_w, ada_b)
    mats_l, mats_c = _dft_mats(S), _dft_mats(lc)
    rope_l = _rope_tables(S)
    rope_c = (jnp.ones((lc, LANES), F32), jnp.zeros((lc, LANES), F32))
    wg, wu, wd = (w.astype(BF16) for w in (moe_w_gate, moe_w_up, moe_w_down))
    for i in range(depth):
        last = i == depth - 1
        j = i // 2
        m_l = [mods[i, :B, None, k * D:(k + 1) * D] for k in range(N_MOD)]
        m_c = [jnp.broadcast_to(mods[i, B, None, None, k * D:(k + 1) * D], (B, 1, D)) for k in range(N_MOD)]
        h_l = _normmod(x, norm1_g[i], m_l[0], m_l[1])
        h_c = _normmod(ctx, norm1_g[i], m_c[0], m_c[1])
        if i % 2 == 0:
            filt = functools.partial(_hyena_filters, w1=hy_w1[j], b1=hy_b1[j], w2=hy_w2[j], b2=hy_b2[j],
                                     w3=hy_w3[j], freq=hy_freq[j])
            y_l, y_c, w_o = _even_mixer(
                h_l, h_c, ev_w_in[j].astype(BF16), ev_w_out[j].astype(BF16), na_q_g[j], na_k_g[j],
                _na_bias(na_rpb[j]), hy_short_w[j], hy_short_b[j], filt(S), filt(lc) if not last else None,
                hy_d[j], mats_l, mats_c, not last)
        else:
            y_l, y_c, w_o = _mla_mixer(h_l, h_c, mla_w_down[j], mla_qa_g[j], mla_kva_g[j], mla_w_uq[j],
                                       mla_w_ukv[j], mla_q_g[j], mla_k_g[j], mla_w_o[j], rope_l, rope_c,
                                       not last)
        x = _gated_residual_mm(y_l, w_o, x, m_l[2], "mixer_out_proj")
        if not last:
            ctx = _gated_residual_mm(y_c, w_o, ctx, m_c[2], "mixer_out_proj")
        x, ctx = _moe(i, x, None if last else ctx, norm2_g[i], m_l, m_c, router_w[i].T, wg, wu, wd)
    return x
```

```python
import functools
import math

import numpy as np
import jax
import jax.numpy as jnp
from jax import lax
from jax.experimental import pallas as pl
from jax.experimental.pallas import tpu as pltpu

F32, BF16, I32, U32 = jnp.float32, jnp.bfloat16, jnp.int32, jnp.uint32
HIGHEST = lax.Precision.HIGHEST

D_MODEL = 2048
DEPTH = 4
GRID_W = 64
N_MOD = 6
NA_HEADS = 8
NA_HEAD_DIM = 128
NA_WIDTH = NA_HEADS * NA_HEAD_DIM
NA_WIN_ROWS = 8
NA_WIN_COLS = 16
HY_WIDTH = D_MODEL - NA_WIDTH
HY_ORDER = 2
HY_EMB = 33
HY_BANDS = (HY_EMB - 1) // 2
HY_HIDDEN = 64
HY_DECAY_SHORT = 0.3
HY_DECAY_LONG = 1.5
HY_DECAY_TARGET = 1e-2
MLA_HEADS = 16
MLA_Q_LORA = 768
MLA_KV_LORA = 512
MLA_NOPE = 128
MLA_ROPE = 64
MLA_V = 128
MLA_QK = MLA_NOPE + MLA_ROPE
MLA_HEAD_PAD = 256
ROPE_BASE = 10000.0
N_EXPERTS = 16
EC_CAPACITY = 2
EXPERT_FF = 1024
EPS = 1e-6
NEG_INF = -1e30

LANES = 128
V7X_VMEM_LIMIT_BYTES = 56 * 1024 * 1024


def _params(n_axes):
    return pltpu.CompilerParams(dimension_semantics=("arbitrary",) * n_axes,
                                vmem_limit_bytes=V7X_VMEM_LIMIT_BYTES)


def _nt(a, b):
    return lax.dot_general(a, b, (((1,), (1,)), ((), ())), preferred_element_type=F32)


def _mm(a, b, *, tm, tn, out_dtype, epi=None, extras=(), n_out=1, order="gij", precision=None, name="mm"):
    a3, b3 = a.ndim == 3, b.ndim == 3
    G = a.shape[0] if a3 else (b.shape[0] if b3 else 1)
    M = a.shape[-2]
    K, N = b.shape[-2:]
    tm, tn = min(tm, M), min(tn, N)
    assert M % tm == 0 and N % tn == 0 and a.shape[-1] >= K
    sizes = dict(g=G, i=M // tm, j=N // tn)
    grid = tuple(sizes[c] for c in order)

    def gij(pid):
        d = dict(zip(order, pid))
        return d["g"], d["i"], d["j"]

    def a_map(*pid):
        g, i, _ = gij(pid)
        return (g, i, 0) if a3 else (i, 0)

    def b_map(*pid):
        g, _, j = gij(pid)
        return (g, 0, j) if b3 else (0, j)

    in_specs = [pl.BlockSpec((None, tm, K) if a3 else (tm, K), a_map),
                pl.BlockSpec((None, K, tn) if b3 else (K, tn), b_map)]
    for _, blk, fn in extras:
        in_specs.append(pl.BlockSpec(blk, lambda *pid, fn=fn: fn(*gij(pid))))

    def body(a_ref, b_ref, *rest):
        extra_refs, o_refs = rest[:len(extras)], rest[len(extras):]
        acc = jnp.dot(a_ref[...], b_ref[...], preferred_element_type=F32, precision=precision)
        outs = acc if epi is None else epi(acc, *extra_refs)
        for o_ref, o in zip(o_refs, outs if n_out > 1 else (outs,)):
            o_ref[...] = o.astype(o_ref.dtype)

    out = pl.pallas_call(
        body, grid=grid, in_specs=in_specs,
        out_specs=[pl.BlockSpec((None, tm, tn), lambda *pid: gij(pid))] * n_out,
        out_shape=[jax.ShapeDtypeStruct((G, M, N), out_dtype)] * n_out,
        compiler_params=_params(3), name=name,
    )(a, b, *[e[0] for e in extras])
    return out if n_out > 1 else out[0]


def _gated_residual_mm(a, w, res, gate, name):
    tm, tn = min(1024, a.shape[1]), min(1024, w.shape[1])
    return _mm(a, w, tm=tm, tn=tn, out_dtype=F32, name=name,
               epi=lambda acc, r_ref, g_ref: r_ref[...] + g_ref[...] * acc,
               extras=[(res, (None, tm, tn), lambda g, i, j: (g, i, j)),
                       (gate, (None, 1, tn), lambda g, i, j: (g, 0, j))])


def _ada_kernel(c_ref, w_ref, b_ref, o_ref):
    c = c_ref[...]
    sc = c * jax.nn.sigmoid(c)
    o_ref[...] = jnp.dot(sc, w_ref[...], preferred_element_type=F32, precision=HIGHEST) + b_ref[...]


def _ada_modulation(c8, ada_w, ada_b):
    depth, d, n = ada_w.shape
    tn = 1024
    return pl.pallas_call(
        _ada_kernel, grid=(depth, n // tn),
        in_specs=[pl.BlockSpec((8, d), lambda l, j: (0, 0)),
                  pl.BlockSpec((None, d, tn), lambda l, j: (l, 0, j)),
                  pl.BlockSpec((None, 1, tn), lambda l, j: (l, 0, j))],
        out_specs=pl.BlockSpec((None, 8, tn), lambda l, j: (l, 0, j)),
        out_shape=jax.ShapeDtypeStruct((depth, 8, n), F32),
        compiler_params=_params(2), name="ada_modulation",
    )(c8, ada_w, ada_b.reshape(depth, 1, n))


def _normmod_kernel(x_ref, g_ref, sh_ref, sc_ref, *rest, with_router):
    x = x_ref[...]
    y = x * lax.rsqrt(jnp.mean(x * x, axis=-1, keepdims=True) + EPS) * g_ref[...]
    h = y * (1.0 + sc_ref[...]) + sh_ref[...]
    if with_router:
        rw_ref, o_ref, lg_ref = rest
        lg_ref[...] = lax.dot_general(rw_ref[...], h, (((1,), (1,)), ((), ())),
                                      preferred_element_type=F32, precision=HIGHEST)
        o_ref[...] = _pack_bf16_pairs(h)
    else:
        (o_ref,) = rest
        o_ref[...] = h.astype(o_ref.dtype)


def _pack_bf16_pairs(h):
    half = h.shape[1] // 2
    bits = pltpu.bitcast(h.astype(BF16).astype(F32), U32)
    return (bits[:, :half] >> 16) | (bits[:, half:] & jnp.uint32(0xFFFF0000))


def _unpack_bf16_pairs(p):
    lo = pltpu.bitcast(p << 16, F32)
    hi = pltpu.bitcast(p & jnp.uint32(0xFFFF0000), F32)
    return jnp.concatenate([lo, hi], axis=1).astype(BF16)


def _normmod(x, g, shift, scale, router_wt=None):
    G, M, D = x.shape
    tm = min(512, M)
    with_router = router_wt is not None
    in_specs = [pl.BlockSpec((None, tm, D), lambda g_, i: (g_, i, 0)),
                pl.BlockSpec((1, D), lambda g_, i: (0, 0)),
                pl.BlockSpec((None, 1, D), lambda g_, i: (g_, 0, 0)),
                pl.BlockSpec((None, 1, D), lambda g_, i: (g_, 0, 0))]
    out_specs = [pl.BlockSpec((None, tm, D), lambda g_, i: (g_, i, 0))]
    out_shape = [jax.ShapeDtypeStruct((G, M, D), BF16)]
    args = [x, g.reshape(1, D), shift, scale]
    if with_router:
        E = router_wt.shape[0]
        out_specs = [pl.BlockSpec((None, tm, D // 2), lambda g_, i: (g_, i, 0))]
        out_shape = [jax.ShapeDtypeStruct((G, M, D // 2), U32)]
        in_specs.append(pl.BlockSpec((E, D), lambda g_, i: (0, 0)))
        out_specs.append(pl.BlockSpec((None, E, tm), lambda g_, i: (g_, 0, i)))
        out_shape.append(jax.ShapeDtypeStruct((G, E, M), F32))
        args.append(router_wt)
    out = pl.pallas_call(
        functools.partial(_normmod_kernel, with_router=with_router), grid=(G, M // tm),
        in_specs=in_specs, out_specs=out_specs, out_shape=out_shape,
        compiler_params=_params(2), name="normmod_router" if with_router else "normmod",
    )(*args)
    return out if with_router else out[0]


def _headnorm_kernel(x_ref, g_ref, o_ref, *, n_heads, head_dim, scale):
    for h in range(n_heads):
        sl = slice(h * head_dim, (h + 1) * head_dim)
        x = x_ref[:, sl].astype(F32)
        y = x * lax.rsqrt(jnp.mean(x * x, axis=-1, keepdims=True) + EPS) * g_ref[...]
        o_ref[:, sl] = (y * scale).astype(o_ref.dtype)


def _headnorm(x, lane_block, g, scale):
    G, M, _ = x.shape
    tm = min(512, M)
    return pl.pallas_call(
        functools.partial(_headnorm_kernel, n_heads=NA_HEADS, head_dim=NA_HEAD_DIM, scale=scale),
        grid=(G, M // tm),
        in_specs=[pl.BlockSpec((None, tm, NA_WIDTH), lambda g_, i: (g_, i, lane_block)),
                  pl.BlockSpec((1, NA_HEAD_DIM), lambda g_, i: (0, 0))],
        out_specs=pl.BlockSpec((None, tm, NA_WIDTH), lambda g_, i: (g_, i, 0)),
        out_shape=jax.ShapeDtypeStruct((G, M, NA_WIDTH), BF16),
        compiler_params=_params(2), name="headnorm",
    )(x, g.reshape(1, NA_HEAD_DIM))


def _attn_kernel(q_ref, k_ref, v_ref, o_ref, *, base2):
    s = _nt(q_ref[...], k_ref[...])
    d = s - jnp.max(s, axis=-1, keepdims=True)
    p = jnp.exp2(d) if base2 else jnp.exp(d)
    l = jnp.sum(p, axis=-1, keepdims=True)
    o = jnp.dot(p.astype(BF16), v_ref[...], preferred_element_type=F32)
    o_ref[...] = (o / l).astype(o_ref.dtype)


def _attention(q, k, v, *, n_heads, dq, dv, v_block, base2=False):
    G, L, _ = q.shape
    return pl.pallas_call(
        functools.partial(_attn_kernel, base2=base2), grid=(G, n_heads),
        in_specs=[pl.BlockSpec((None, L, dq), lambda g, h: (g, 0, h)),
                  pl.BlockSpec((None, L, dq), lambda g, h: (g, 0, h)),
                  pl.BlockSpec((None, L, dv), lambda g, h: (g, 0, v_block(h)))],
        out_specs=pl.BlockSpec((None, L, dv), lambda g, h: (g, 0, h)),
        out_shape=jax.ShapeDtypeStruct((G, L, n_heads * dv), BF16),
        compiler_params=_params(2), name="ctx_attention",
    )(q, k, v)


MLA_KEY_BLOCK = 512


def _mla_attn_kernel(q_ref, kc_ref, vc_ref, kl_ref, vl_ref, o_ref):
    q = q_ref[...]

    def probs(s, m):
        return jnp.exp2((s - m).astype(BF16))

    s = _nt(q, kc_ref[...])
    m = jnp.max(s, axis=-1, keepdims=True)
    acc = jnp.dot(probs(s, m), vc_ref[...], preferred_element_type=F32)
    for j in range(kl_ref.shape[0] // MLA_KEY_BLOCK):
        rows = slice(j * MLA_KEY_BLOCK, (j + 1) * MLA_KEY_BLOCK)
        s = _nt(q, kl_ref[rows, :])
        m_new = jnp.maximum(m, jnp.max(s, axis=-1, keepdims=True))
        acc = jnp.exp2(m - m_new) * acc + jnp.dot(probs(s, m_new), vl_ref[rows, :], preferred_element_type=F32)
        m = m_new
    o_ref[...] = (acc[:, :MLA_V] / acc[:, MLA_V:]).astype(o_ref.dtype)


def _mla_attention(q, k_c, v_c, k_l, v_l):
    G, S, _ = q.shape
    lc = k_c.shape[1]
    tq = min(512, S)
    assert S % MLA_KEY_BLOCK == 0
    dq = MLA_HEAD_PAD
    head = lambda g, h, i: (g, 0, h)
    return pl.pallas_call(
        _mla_attn_kernel, grid=(G, MLA_HEADS, S // tq),
        in_specs=[pl.BlockSpec((None, tq, dq), lambda g, h, i: (g, i, h)),
                  pl.BlockSpec((None, lc, dq), head), pl.BlockSpec((None, lc, dq), head),
                  pl.BlockSpec((None, S, dq), head), pl.BlockSpec((None, S, dq), head)],
        out_specs=pl.BlockSpec((None, tq, MLA_V), lambda g, h, i: (g, i, h)),
        out_shape=jax.ShapeDtypeStruct((G, S, MLA_HEADS * MLA_V), BF16),
        compiler_params=_params(3), name="mla_attention",
    )(q, k_c, v_c, k_l, v_l)


NA_ROWS_PER_STEP = 8


def _na_kernel(q_ref, k_ref, v_ref, kc_ref, vc_ref, bias_ref, o_ref, *, n_rows):
    rb = pl.program_id(2)
    kc = kc_ref[...]
    vc = vc_ref[...]
    win = NA_WIN_ROWS * GRID_W
    q = q_ref[...]
    starts, s_w = [], []
    for i in range(NA_ROWS_PER_STEP):
        r = rb * NA_ROWS_PER_STEP + i
        r0 = jnp.clip(r - NA_WIN_ROWS // 2, 0, n_rows - NA_WIN_ROWS)
        starts.append(pl.multiple_of(r0 * GRID_W, GRID_W))
        s_w.append(_nt(q[i * GRID_W:(i + 1) * GRID_W, :], k_ref[pl.ds(starts[i], win), :]) + bias_ref[r - r0])
    s_w = jnp.concatenate(s_w, axis=0)
    s_c = _nt(q, kc)
    m = jnp.maximum(jnp.max(s_w, axis=-1, keepdims=True), jnp.max(s_c, axis=-1, keepdims=True))
    p_w = jnp.exp(s_w - m)
    p_c = jnp.exp(s_c - m)
    l = jnp.sum(p_w, axis=-1, keepdims=True) + jnp.sum(p_c, axis=-1, keepdims=True)
    p_w = p_w.astype(BF16)
    o_w = [jnp.dot(p_w[i * GRID_W:(i + 1) * GRID_W, :], v_ref[pl.ds(starts[i], win), :],
                   preferred_element_type=F32) for i in range(NA_ROWS_PER_STEP)]
    o = jnp.concatenate(o_w, axis=0) + jnp.dot(p_c.astype(BF16), vc, preferred_element_type=F32)
    o_ref[...] = (o / l).astype(o_ref.dtype)


def _na_attention(q, k, proj, kc, proj_c, bias):
    G, S, _ = q.shape
    lc = kc.shape[1]
    n_rows = S // GRID_W
    assert n_rows >= NA_WIN_ROWS and n_rows % NA_ROWS_PER_STEP == 0
    tq = NA_ROWS_PER_STEP * GRID_W
    hd = NA_HEAD_DIM
    v_off = 2 * NA_HEADS
    return pl.pallas_call(
        functools.partial(_na_kernel, n_rows=n_rows),
        grid=(G, NA_HEADS, n_rows // NA_ROWS_PER_STEP),
        in_specs=[pl.BlockSpec((None, tq, hd), lambda g, h, r: (g, r, h)),
                  pl.BlockSpec((None, S, hd), lambda g, h, r: (g, 0, h)),
                  pl.BlockSpec((None, S, hd), lambda g, h, r: (g, 0, v_off + h)),
                  pl.BlockSpec((None, lc, hd), lambda g, h, r: (g, 0, h)),
                  pl.BlockSpec((None, lc, hd), lambda g, h, r: (g, 0, v_off + h)),
                  pl.BlockSpec((None, NA_WIN_ROWS, GRID_W, NA_WIN_ROWS * GRID_W),
                               lambda g, h, r: (h, 0, 0, 0))],
        out_specs=pl.BlockSpec((None, tq, hd), lambda g, h, r: (g, r, h)),
        out_shape=jax.ShapeDtypeStruct((G, S, NA_WIDTH), BF16),
        compiler_params=_params(3), name="na_attention",
    )(q, k, proj, kc, proj_c, bias)


def _na_bias(rpb):
    cols = np.arange(GRID_W)
    col_start = np.clip(cols - NA_WIN_COLS // 2, 0, GRID_W - NA_WIN_COLS)
    in_win = (cols[None, :] >= col_start[:, None]) & (cols[None, :] < col_start[:, None] + NA_WIN_COLS)
    wr, wc = NA_WIN_ROWS, NA_WIN_COLS
    a = jnp.stack([rpb[:, wr - 1 - v:2 * wr - 1 - v, :] for v in range(wr)], axis=1).astype(F32)
    pad = GRID_W - wc
    ap = jnp.pad(a, ((0, 0), (0, 0), (0, 0), (pad, pad)))
    b = jnp.stack([ap[..., GRID_W - 1 - q:2 * GRID_W - 1 - q] for q in range(GRID_W)], axis=3)
    b = jnp.transpose(b, (0, 1, 3, 2, 4))
    b = jnp.where(in_win[None, None, :, None, :], b, NEG_INF)
    return b.reshape(rpb.shape[0], wr, GRID_W, wr * GRID_W)


def _shortconv_kernel(x_ref, w_ref, b_ref, o_ref):
    x = x_ref[...].astype(F32)
    n = x.shape[0]
    row = lax.broadcasted_iota(I32, x.shape, 0)
    prev = jnp.where(row == 0, 0.0, pltpu.roll(x, 1, 0))
    nxt = jnp.where(row == n - 1, 0.0, pltpu.roll(x, n - 1, 0))
    w = w_ref[...]
    o_ref[...] = (w[0:1] * prev + w[1:2] * x + w[2:3] * nxt + b_ref[...]).astype(o_ref.dtype)


def _shortconv(proj, lane_off, w, b):
    G, L, _ = proj.shape
    C = w.shape[1]
    tc = 256
    off = lane_off // tc
    return pl.pallas_call(
        _shortconv_kernel, grid=(G, C // tc),
        in_specs=[pl.BlockSpec((None, L, tc), lambda g, j: (g, 0, off + j)),
                  pl.BlockSpec((3, tc), lambda g, j: (0, j)),
                  pl.BlockSpec((1, tc), lambda g, j: (0, j))],
        out_specs=pl.BlockSpec((None, L, tc), lambda g, j: (g, 0, j)),
        out_shape=jax.ShapeDtypeStruct((G, L, C), BF16),
        compiler_params=_params(2), name="hyena_shortconv",
    )(proj, w, b.reshape(1, C))


def _hyfilt_kernel(z_ref, w1_ref, b1_ref, w2_ref, b2_ref, fr_ref, w3f_ref, w3b_ref, dl_ref, os_ref, od_ref,
                   hid_s):
    z = z_ref[...]

    @pl.when((pl.program_id(0) == 0) & (pl.program_id(1) == 0))
    def _():
        fr = fr_ref[...]
        hid = jnp.sin(fr * (jnp.dot(z, w1_ref[...], preferred_element_type=F32, precision=HIGHEST) + b1_ref[...]))
        hid_s[...] = jnp.sin(fr * (jnp.dot(hid, w2_ref[...], preferred_element_type=F32, precision=HIGHEST)
                                   + b2_ref[...]))

    hid = hid_s[...]
    decay = jnp.exp(-z[:, 0:1] * dl_ref[...])

    def filt(w3_ref):
        f = jnp.dot(hid, w3_ref[...], preferred_element_type=F32, precision=HIGHEST) * decay
        return f * lax.rsqrt(jnp.sum(f * f, axis=0, keepdims=True) + EPS)

    ff, fb = filt(w3f_ref), filt(w3b_ref)
    os_ref[...] = (ff + fb).astype(os_ref.dtype)
    od_ref[...] = (ff - fb).astype(od_ref.dtype)


def _hyena_filters(L, w1, b1, w2, b2, w3, freq):
    t = jnp.linspace(0.0, 1.0, L, dtype=F32)[:, None]
    wv = 2.0 * math.pi * jnp.arange(L, dtype=F32)[:, None] / L
    bands = jnp.linspace(1e-4, HY_BANDS - 1, HY_BANDS, dtype=F32)[None, :]
    z = jnp.concatenate([t, jnp.cos(bands * wv), -jnp.sin(bands * wv)], axis=-1)
    z = jnp.pad(z, ((0, 0), (0, LANES - HY_EMB)))
    hp = LANES - HY_HIDDEN
    w1p = jnp.pad(w1, ((0, LANES - HY_EMB), (0, hp)))
    w2p = jnp.pad(w2, ((0, hp), (0, hp)))
    w3p = jnp.pad(w3, ((0, hp), (0, 0)))
    b1p = jnp.pad(b1, (0, hp)).reshape(1, LANES)
    b2p = jnp.pad(b2, (0, hp)).reshape(1, LANES)
    frp = jnp.pad(freq, (0, hp)).reshape(1, LANES)
    d_lo = math.log(HY_DECAY_TARGET) / HY_DECAY_LONG
    d_hi = math.log(HY_DECAY_TARGET) / HY_DECAY_SHORT
    deltas = jnp.abs(jnp.linspace(d_lo, d_hi, HY_WIDTH, dtype=F32)).reshape(1, HY_WIDTH)
    tc = 256
    nct = HY_WIDTH // tc
    const = lambda o, j: (0, 0)
    out = jax.ShapeDtypeStruct((L, HY_ORDER * HY_WIDTH), BF16)
    return pl.pallas_call(
        _hyfilt_kernel, grid=(HY_ORDER, nct),
        in_specs=[pl.BlockSpec((L, LANES), const), pl.BlockSpec((LANES, LANES), const),
                  pl.BlockSpec((1, LANES), const), pl.BlockSpec((LANES, LANES), const),
                  pl.BlockSpec((1, LANES), const), pl.BlockSpec((1, LANES), const),
                  pl.BlockSpec((LANES, tc), lambda o, j: (0, 2 * o * nct + j)),
                  pl.BlockSpec((LANES, tc), lambda o, j: (0, (2 * o + 1) * nct + j)),
                  pl.BlockSpec((1, tc), lambda o, j: (0, j))],
        out_specs=[pl.BlockSpec((L, tc), lambda o, j: (0, o * nct + j))] * 2,
        out_shape=[out, out], scratch_shapes=[pltpu.VMEM((L, LANES), F32)],
        compiler_params=_params(2), name="hyena_filters",
    )(z, w1p, b1p, w2p, b2p, frp, w3p, w3p, deltas)


def _dft_tables(L):
    A = 1 << (int(math.log2(L)) // 2)
    period = 4 * L
    k = np.arange(L, dtype=np.int64)[:, None]
    ang_a = 2.0 * np.pi * (((2 * k + 1) * (A * np.arange(L // A, dtype=np.int64)[None, :])) % period) / period
    ang_b = 2.0 * np.pi * (((2 * k + 1) * np.arange(A, dtype=np.int64)[None, :]) % period) / period
    return tuple(np.asarray(f(x), np.float32) for x in (ang_a, ang_b) for f in (np.cos, np.sin))


def _dft_mats(L):
    ca, sa, cb, sb = (jnp.asarray(t) for t in _dft_tables(L))
    fc = (ca[:, :, None] * cb[:, None, :] - sa[:, :, None] * sb[:, None, :]).reshape(L, L)
    fs = (sa[:, :, None] * cb[:, None, :] + ca[:, :, None] * sb[:, None, :]).reshape(L, L)
    cat, sat, cbt, sbt = ca.T, sa.T, cb.T, sb.T
    fct = (cat[:, None, :] * cbt[None, :, :] - sat[:, None, :] * sbt[None, :, :]).reshape(L, L)
    fst = (sat[:, None, :] * cbt[None, :, :] + cat[:, None, :] * sbt[None, :, :]).reshape(L, L)
    return tuple(m.astype(BF16) for m in (fc, fs, fct, fst))


def _dft_fwd_kernel(fc_ref, fs_ref, u_ref, hc_ref, hs_ref, yc_ref, ys_ref):
    u = u_ref[...]
    xc = jnp.dot(fc_ref[...], u, preferred_element_type=F32)
    xs = jnp.dot(fs_ref[...], u, preferred_element_type=F32)
    hc, hs = hc_ref[...], hs_ref[...]
    yc_ref[...] = (xc * hc - xs * hs).astype(yc_ref.dtype)
    ys_ref[...] = (xc * hs + xs * hc).astype(ys_ref.dtype)


def _dft_fwd(fc, fs, u, u_block, hc, hs, h_block):
    G, L, _ = u.shape
    C = HY_WIDTH
    tr, tc = min(512, L), 512
    nct = C // tc
    spec_f = pl.BlockSpec((tr, L), lambda k, g, j: (k, 0))
    spec_h = pl.BlockSpec((None, tr, tc), lambda k, g, j: (0, k, h_block * nct + j))
    spec_y = pl.BlockSpec((None, tr, tc), lambda k, g, j: (g, k, j))
    out = jax.ShapeDtypeStruct((G, L, C), BF16)
    return pl.pallas_call(
        _dft_fwd_kernel, grid=(L // tr, G, nct),
        in_specs=[spec_f, spec_f, pl.BlockSpec((None, L, tc), lambda k, g, j: (g, 0, u_block * nct + j)),
                  spec_h, spec_h],
        out_specs=[spec_y, spec_y], out_shape=[out, out],
        compiler_params=_params(3), name="hyena_dft_fwd",
    )(fc, fs, u, hc, hs)


def _dft_inv_kernel(fct_ref, fst_ref, yc_ref, ys_ref, u_ref, x_ref, d_ref, o_ref, *, inv_scale):
    y = (jnp.dot(fct_ref[...], yc_ref[...], preferred_element_type=F32)
         + jnp.dot(fst_ref[...], ys_ref[...], preferred_element_type=F32)) * inv_scale
    y = y + u_ref[...].astype(F32) * d_ref[...]
    o_ref[...] = (x_ref[...].astype(F32) * y).astype(o_ref.dtype)


def _dft_inv(fct, fst, yc, ys, u, u_block, gate, gate_block, d):
    G, L, C = yc.shape
    tr, tc = min(512, L), 512
    nct = C // tc
    spec_f = pl.BlockSpec((tr, L), lambda n, g, j: (n, 0))
    spec_y = pl.BlockSpec((None, L, tc), lambda n, g, j: (g, 0, j))
    return pl.pallas_call(
        functools.partial(_dft_inv_kernel, inv_scale=1.0 / L),
        grid=(L // tr, G, nct),
        in_specs=[spec_f, spec_f, spec_y, spec_y,
                  pl.BlockSpec((None, tr, tc), lambda n, g, j: (g, n, u_block * nct + j)),
                  pl.BlockSpec((None, tr, tc), lambda n, g, j: (g, n, gate_block * nct + j)),
                  pl.BlockSpec((1, tc), lambda n, g, j: (0, j))],
        out_specs=pl.BlockSpec((None, tr, tc), lambda n, g, j: (g, n, j)),
        out_shape=jax.ShapeDtypeStruct((G, L, C), BF16),
        compiler_params=_params(3), name="hyena_dft_inv",
    )(fct, fst, yc, ys, u, gate, d.reshape(1, C))


def _hyena(proj, short_w, short_b, hsum, hdiff, hy_d, mats):
    fc, fs, fct, fst = mats
    sc = _shortconv(proj, 3 * NA_WIDTH, short_w, short_b)
    hc = _mm(fc, hsum, tm=512, tn=512, out_dtype=F32, name="hyena_filter_dft")
    hs = _mm(fs, hdiff, tm=512, tn=512, out_dtype=F32, name="hyena_filter_dft")
    z, z_block = sc, 0
    for o in range(HY_ORDER):
        yc, ys = _dft_fwd(fc, fs, z, z_block, hc, hs, o)
        z = _dft_inv(fct, fst, yc, ys, z, z_block, sc, o + 1, hy_d[o])
        z_block = 0
    return z


def _rope_rotate(y):
    lane = lax.broadcasted_iota(I32, y.shape, 1)
    first = (lane % 32) < 16
    return jnp.where(first, -pltpu.roll(y, LANES - 16, 1), pltpu.roll(y, 16, 1))


def _mla_q_epilogue(acc, g_ref, cos_ref, sin_ref, *, scale):
    cos, sin = cos_ref[...], sin_ref[...]
    g = g_ref[...]
    pieces = []
    for h in range(acc.shape[1] // MLA_HEAD_PAD):
        lo = h * MLA_HEAD_PAD
        xn = acc[:, lo:lo + LANES]
        xr = acc[:, lo + LANES:lo + 2 * LANES]
        ss = jnp.sum(xn * xn + xr * xr, axis=-1, keepdims=True)
        rs = lax.rsqrt(ss * (1.0 / MLA_QK) + EPS) * scale
        yr = xr * rs * g[:, LANES:]
        pieces += [xn * rs * g[:, :LANES], yr * cos + _rope_rotate(yr) * sin]
    return jnp.concatenate(pieces, axis=1)


def _mla_kv_epilogue(acc, kr_ref, g_ref, cos_ref, sin_ref):
    cos, sin = cos_ref[...], sin_ref[...]
    g = g_ref[...]
    kr = kr_ref[...].astype(F32)
    sr = jnp.sum(kr * kr, axis=-1, keepdims=True)
    krg = kr * g[:, LANES:]
    krot = krg * cos + _rope_rotate(krg) * sin
    ones = jnp.ones((acc.shape[0], LANES), F32)
    ks, vs = [], []
    for h in range(acc.shape[1] // MLA_HEAD_PAD):
        lo = h * MLA_HEAD_PAD
        kn = acc[:, lo:lo + LANES]
        rs = lax.rsqrt((jnp.sum(kn * kn, axis=-1, keepdims=True) + sr) * (1.0 / MLA_QK) + EPS)
        ks += [kn * rs * g[:, :LANES], krot * rs]
        vs += [acc[:, lo + LANES:lo + 2 * LANES], ones]
    return jnp.concatenate(ks, axis=1), jnp.concatenate(vs, axis=1)


def _rope_tables(S):
    t = jnp.arange(S)
    half = MLA_ROPE // 2
    inv = ROPE_BASE ** (-jnp.arange(0, half, 2, dtype=F32) / half)
    ang_r = (t // GRID_W).astype(F32)[:, None] * inv[None, :]
    ang_c = (t % GRID_W).astype(F32)[:, None] * inv[None, :]
    ang = jnp.concatenate([ang_r, ang_r, ang_c, ang_c], axis=-1)
    pad = LANES - MLA_ROPE
    return (jnp.pad(jnp.cos(ang), ((0, 0), (0, pad)), constant_values=1.0),
            jnp.pad(jnp.sin(ang), ((0, 0), (0, pad))))


def _lane_cumsum(x):
    n = x.shape[1]
    tri = (lax.broadcasted_iota(I32, (LANES, LANES), 0) <= lax.broadcasted_iota(I32, (LANES, LANES), 1))
    tri = jnp.where(tri, 1.0, 0.0).astype(BF16)
    off = jnp.zeros((x.shape[0], 1), F32)
    outs = []
    for j in range(n // LANES):
        c = jnp.dot(x[:, j * LANES:(j + 1) * LANES], tri, preferred_element_type=F32) + off
        outs.append(c)
        off = c[:, LANES - 1:LANES]
    return jnp.concatenate(outs, axis=1)


ROUTE_SLOT_CHUNK = 64


def _route_kernel(lg_ref, idx_ref, gt_ref, aff_s, slot_s, *, cap, chunk):
    lg = lg_ref[...]
    n_exp, n_tok = lg.shape
    ex = jnp.exp(lg - jnp.max(lg, axis=0, keepdims=True))
    aff = ex / jnp.sum(ex, axis=0, keepdims=True)
    bits = pltpu.bitcast(aff, I32)

    def search(i, v):
        cand = v | jnp.left_shift(jnp.int32(1), 30 - i)
        cnt = jnp.sum(jnp.where(bits >= cand, 1.0, 0.0), axis=1, keepdims=True)
        return jnp.where(cnt >= cap, cand, v)

    thr = lax.fori_loop(0, 31, search, jnp.zeros((n_exp, 1), I32))
    gt = bits > thr
    eq = bits == thr
    need = cap - jnp.sum(jnp.where(gt, 1.0, 0.0), axis=1, keepdims=True)
    eq_rank = _lane_cumsum(jnp.where(eq, 1.0, 0.0).astype(BF16))
    sel = jnp.where(gt, 1.0, jnp.where(eq, jnp.where(eq_rank <= need, 1.0, 0.0), 0.0))
    slot = _lane_cumsum(sel.astype(BF16)) * sel
    aff_s[...] = aff
    slot_s[...] = slot
    tok = lax.broadcasted_iota(I32, (chunk, n_tok), 1).astype(F32)

    def per_expert(e, carry):
        srow = slot_s[pl.ds(e, 1), :]
        arow = aff_s[pl.ds(e, 1), :]

        def per_chunk(c, carry2):
            p0 = pl.multiple_of(c * chunk, chunk)
            want = (lax.broadcasted_iota(I32, (chunk, 1), 0) + (p0 + 1)).astype(F32)
            hit = srow == want
            idx_ref[e, pl.ds(p0, chunk), :] = jnp.sum(
                jnp.where(hit, tok, 0.0), axis=1, keepdims=True).astype(I32)
            gt_ref[e, pl.ds(p0, chunk), :] = jnp.sum(
                jnp.where(hit, arow, 0.0), axis=1, keepdims=True)
            return carry2

        return lax.fori_loop(0, cap // chunk, per_chunk, carry)

    lax.fori_loop(0, n_exp, per_expert, 0)


def _route(logits_t, cap):
    G, E, N = logits_t.shape
    chunk = min(ROUTE_SLOT_CHUNK, cap)
    assert cap % chunk == 0 and N % LANES == 0
    kern = functools.partial(_route_kernel, cap=cap, chunk=chunk)
    spec_o = pl.BlockSpec((None, E, cap, 1), lambda g: (g, 0, 0, 0))
    return pl.pallas_call(
        kern, grid=(G,), in_specs=[pl.BlockSpec((None, E, N), lambda g: (g, 0, 0))],
        out_specs=[spec_o, spec_o],
        out_shape=[jax.ShapeDtypeStruct((G, E, cap, 1), I32), jax.ShapeDtypeStruct((G, E, cap, 1), F32)],
        scratch_shapes=[pltpu.VMEM((E, N), F32), pltpu.VMEM((E, N), F32)],
        compiler_params=_params(1), name="moe_route",
    )(logits_t)


def _gather_kernel(idx_ref, h_ref, o_ref, *, cap, n_exp):
    base = (pl.program_id(0) * n_exp + pl.program_id(1)) * cap

    def body(p, carry):
        r = idx_ref[base + p]
        o_ref[pl.ds(p, 1), :] = h_ref[pl.ds(r, 1), :]
        return carry

    lax.fori_loop(0, cap, body, 0, unroll=8)


def _gather(idx_flat, h32, n_exp, cap):
    G, N, W = h32.shape
    return pl.pallas_call(
        functools.partial(_gather_kernel, cap=cap, n_exp=n_exp),
        grid_spec=pltpu.PrefetchScalarGridSpec(
            num_scalar_prefetch=1, grid=(G, n_exp),
            in_specs=[pl.BlockSpec((None, N, W), lambda g, e, idx: (g, 0, 0))],
            out_specs=pl.BlockSpec((None, None, cap, W), lambda g, e, idx: (g, e, 0, 0))),
        out_shape=jax.ShapeDtypeStruct((G, n_exp, cap, W), h32.dtype),
        compiler_params=_params(2), name="moe_gather",
    )(idx_flat, h32)


def _swiglu(x32, slot_gate, feat_gate, wg_ref, wu_ref, wd_ref):
    x = _unpack_bf16_pairs(x32)
    a = jnp.dot(x, wg_ref[...], preferred_element_type=F32)
    u = jnp.dot(x, wu_ref[...], preferred_element_type=F32)
    h = (a * jax.nn.sigmoid(a) * u).astype(BF16)
    return jnp.dot(h, wd_ref[...], preferred_element_type=F32) * slot_gate * feat_gate


def _ffn_kernel(*refs, n_lat, with_ctx):
    if with_ctx:
        xl_ref, gl_ref, fl_ref, xc_ref, gc_ref, fc_ref, wg_ref, wu_ref, wd_ref, yl_ref, yc_ref = refs
    else:
        xl_ref, gl_ref, fl_ref, wg_ref, wu_ref, wd_ref, yl_ref = refs
    step = pl.program_id(1)

    @pl.when(step < n_lat)
    def _():
        yl_ref[...] = _swiglu(xl_ref[...], gl_ref[...], fl_ref[...], wg_ref, wu_ref, wd_ref)

    if with_ctx:
        @pl.when(step == n_lat)
        def _():
            gb, cc, w = xc_ref.shape
            y = _swiglu(xc_ref[...].reshape(gb * cc, w), gc_ref[...].reshape(gb * cc, 1), fc_ref[...],
                        wg_ref, wu_ref, wd_ref)
            yc_ref[...] = y.reshape(gb, cc, y.shape[1])


def _expert_ffn(layer, wg, wu, wd, lat, ctx=None):
    xl, gl, fl = lat
    G, E, C, W = xl.shape
    D, FF = wg.shape[-2:]
    lat_slot = lambda e, s: (jnp.minimum(s, G - 1), e, 0, 0)
    w_idx = lambda e, s: (layer, e, 0, 0)
    in_specs = [pl.BlockSpec((None, None, C, W), lat_slot), pl.BlockSpec((None, None, C, 1), lat_slot),
                pl.BlockSpec((None, 1, D), lambda e, s: (jnp.minimum(s, G - 1), 0, 0))]
    out_specs = [pl.BlockSpec((None, None, C, D), lat_slot)]
    out_shape = [jax.ShapeDtypeStruct((G, E, C, D), F32)]
    args = [xl, gl, fl]
    if ctx is not None:
        xc, gc, fc = ctx
        Gc, _, Cc, _ = xc.shape
        ctx_slot = lambda e, s: (0, e, 0, 0)
        in_specs += [pl.BlockSpec((Gc, None, Cc, W), ctx_slot), pl.BlockSpec((Gc, None, Cc, 1), ctx_slot),
                     pl.BlockSpec((1, D), lambda e, s: (0, 0))]
        out_specs.append(pl.BlockSpec((Gc, None, Cc, D), ctx_slot))
        out_shape.append(jax.ShapeDtypeStruct((Gc, E, Cc, D), F32))
        args += [xc, gc, fc]
    in_specs += [pl.BlockSpec((None, None, D, FF), w_idx), pl.BlockSpec((None, None, D, FF), w_idx),
                 pl.BlockSpec((None, None, FF, D), w_idx)]
    return pl.pallas_call(
        functools.partial(_ffn_kernel, n_lat=G, with_ctx=ctx is not None),
        grid=(E, G + (ctx is not None)), in_specs=in_specs, out_specs=out_specs, out_shape=out_shape,
        compiler_params=_params(2), name="moe_expert_ffn",
    )(*args, wg, wu, wd)


COMBINE_ROWS = 8


def _combine_kernel(idx_ref, y_ref, x_ref, o_ref, *, cap, n_exp):
    e = pl.program_id(2)
    base = (pl.program_id(0) * n_exp + e) * cap

    @pl.when(e == 0)
    def _():
        o_ref[...] = x_ref[...]

    def body(i, carry):
        p0 = pl.multiple_of(i * COMBINE_ROWS, COMBINE_ROWS)
        rows = [idx_ref[base + p0 + k] for k in range(COMBINE_ROWS)]
        vals = [o_ref[pl.ds(rows[k], 1), :] + y_ref[pl.ds(p0 + k, 1), :] for k in range(COMBINE_ROWS)]
        for k in range(COMBINE_ROWS):
            o_ref[pl.ds(rows[k], 1), :] = vals[k]
        return carry

    lax.fori_loop(0, cap // COMBINE_ROWS, body, 0)


def _combine(idx_flat, y, x):
    G, E, C, D = y.shape
    N = x.shape[1]
    tc = 1024
    assert C % COMBINE_ROWS == 0
    return pl.pallas_call(
        functools.partial(_combine_kernel, cap=C, n_exp=E),
        grid_spec=pltpu.PrefetchScalarGridSpec(
            num_scalar_prefetch=1, grid=(G, D // tc, E),
            in_specs=[pl.BlockSpec((None, None, C, tc), lambda g, j, e, idx: (g, e, 0, j)),
                      pl.BlockSpec((None, N, tc), lambda g, j, e, idx: (g, 0, j),
                                   pipeline_mode=pl.Buffered(1))],
            out_specs=pl.BlockSpec((None, N, tc), lambda g, j, e, idx: (g, 0, j))),
        out_shape=jax.ShapeDtypeStruct((G, N, D), F32),
        compiler_params=_params(3), name="moe_combine",
    )(idx_flat, y, x)


def _route_and_gather(x, g2, shift, scale, router_wt):
    G, N, _ = x.shape
    E = router_wt.shape[0]
    cap = max(1, (EC_CAPACITY * N) // E)
    h32, logits_t = _normmod(x, g2, shift, scale, router_wt)
    idx, gates = _route(logits_t, cap)
    idx_flat = idx.reshape(G * E * cap)
    return idx_flat, gates, _gather(idx_flat, h32, E, cap)


def _moe(layer, x, ctx, g2, m_l, m_c, router_wt, wg, wu, wd):
    il, gl, xl = _route_and_gather(x, g2, m_l[3], m_l[4], router_wt)
    if ctx is None:
        (yl,) = _expert_ffn(layer, wg, wu, wd, (xl, gl, m_l[5]))
        return _combine(il, yl, x), None
    ic, gc, xc = _route_and_gather(ctx, g2, m_c[3], m_c[4], router_wt)
    yl, yc = _expert_ffn(layer, wg, wu, wd, (xl, gl, m_l[5]), (xc, gc, m_c[5][0]))
    return _combine(il, yl, x), _combine(ic, yc, ctx)


def _even_mixer(h_l, h_c, w_in, w_out, q_g, k_g, bias, short_w, short_b, filt_l, filt_c, hy_d,
                mats_l, mats_c, need_ctx):
    scale = NA_HEAD_DIM ** -0.5
    proj_l = _mm(h_l, w_in, tm=1024, tn=1024, out_dtype=BF16, name="even_in_proj")
    proj_c = _mm(h_c, w_in, tm=1024, tn=1024, out_dtype=BF16, name="even_in_proj")
    q_l = _headnorm(proj_l, 0, q_g, scale)
    k_l = _headnorm(proj_l, 1, k_g, 1.0)
    k_c = _headnorm(proj_c, 1, k_g, 1.0)
    na_l = _na_attention(q_l, k_l, proj_l, k_c, proj_c, bias)
    hy_l = _hyena(proj_l, short_w, short_b, *filt_l, hy_d, mats_l)
    y_l = jnp.concatenate([na_l, hy_l], axis=-1)
    y_c = None
    if need_ctx:
        q_c = _headnorm(proj_c, 0, q_g, scale)
        na_c = _attention(q_c, k_c, proj_c, n_heads=NA_HEADS, dq=NA_HEAD_DIM, dv=NA_HEAD_DIM,
                          v_block=lambda h: 2 * NA_HEADS + h)
        hy_c = _hyena(proj_c, short_w, short_b, *filt_c, hy_d, mats_c)
        y_c = jnp.concatenate([na_c, hy_c], axis=-1)
    return y_l, y_c, w_out


def _rms_rows(acc, g_ref, width):
    x = acc[:, :width]
    y = x * lax.rsqrt(jnp.mean(x * x, axis=-1, keepdims=True) + EPS) * g_ref[...]
    return y if width == acc.shape[1] else jnp.concatenate([y, acc[:, width:]], axis=1)


def _mla_project(h, w_dq, w_dkv, qa_g, kva_g, w_uq, w_ukv, q_g, k_g, cos, sin, with_q):
    scale = MLA_QK ** -0.5 * math.log2(math.e)
    kvw = w_dkv.shape[1]
    tm = min(1024, h.shape[1])
    const = lambda g, i, j: (0, 0)
    rope = [(cos, (tm, LANES), lambda g, i, j: (i, 0)), (sin, (tm, LANES), lambda g, i, j: (i, 0))]
    dkv = _mm(h, w_dkv, tm=tm, tn=kvw, out_dtype=BF16, name="mla_down_kv",
              epi=lambda acc, g_ref: _rms_rows(acc, g_ref, MLA_KV_LORA),
              extras=[(kva_g.reshape(1, -1), (1, MLA_KV_LORA), const)])
    kr_block = MLA_KV_LORA // LANES
    k, v = _mm(dkv, w_ukv, tm=tm, tn=1024, out_dtype=BF16, n_out=2, name="mla_up_kv", epi=_mla_kv_epilogue,
               extras=[(dkv, (None, tm, LANES), lambda g, i, j: (g, i, kr_block)),
                       (k_g, (1, MLA_HEAD_PAD), const)] + rope)
    q = None
    if with_q:
        qa = _mm(h, w_dq, tm=tm, tn=MLA_Q_LORA, out_dtype=BF16, name="mla_down_q",
                 epi=lambda acc, g_ref: _rms_rows(acc, g_ref, MLA_Q_LORA),
                 extras=[(qa_g.reshape(1, -1), (1, MLA_Q_LORA), const)])
        q = _mm(qa, w_uq, tm=tm, tn=1024, out_dtype=BF16, name="mla_up_q",
                epi=functools.partial(_mla_q_epilogue, scale=scale),
                extras=[(q_g, (1, MLA_HEAD_PAD), const)] + rope)
    return q, k, v


def _pad_heads(w, head_w):
    k = w.shape[0]
    w = w.reshape(k, MLA_HEADS, head_w)
    return jnp.pad(w, ((0, 0), (0, 0), (0, MLA_HEAD_PAD - head_w))).reshape(k, MLA_HEADS * MLA_HEAD_PAD)


def _mla_mixer(h_l, h_c, w_down, qa_g, kva_g, w_uq, w_ukv, q_g, k_g, w_o, rope_l, rope_c, need_ctx):
    w_dq = w_down[:, :MLA_Q_LORA].astype(BF16)
    w_dkv = jnp.pad(w_down[:, MLA_Q_LORA:], ((0, 0), (0, LANES - MLA_ROPE))).astype(BF16)
    w_uq_p = _pad_heads(w_uq, MLA_QK).astype(BF16)
    w_ukv_b = w_ukv.astype(BF16)
    qg = jnp.pad(q_g, (0, MLA_HEAD_PAD - MLA_QK)).reshape(1, MLA_HEAD_PAD)
    kg = jnp.pad(k_g, (0, MLA_HEAD_PAD - MLA_QK)).reshape(1, MLA_HEAD_PAD)
    proj = functools.partial(_mla_project, w_dq=w_dq, w_dkv=w_dkv, qa_g=qa_g, kva_g=kva_g, w_uq=w_uq_p,
                             w_ukv=w_ukv_b, q_g=qg, k_g=kg)
    q_l, k_l, v_l = proj(h_l, cos=rope_l[0], sin=rope_l[1], with_q=True)
    q_c, k_c, v_c = proj(h_c, cos=rope_c[0], sin=rope_c[1], with_q=need_ctx)
    y_l = _mla_attention(q_l, k_c, v_c, k_l, v_l)
    y_c = None
    if need_ctx:
        y_c = _attention(q_c, k_c, v_c, n_heads=MLA_HEADS, dq=MLA_HEAD_PAD, dv=MLA_V,
                         v_block=lambda h: 2 * h, base2=True)
    return y_l, y_c, w_o.astype(BF16)


def kernel(x, c, ctx, c_ctx, ada_w, ada_b, norm1_g, norm2_g, router_w, moe_w_gate, moe_w_up, moe_w_down, ev_w_in, ev_w_out, na_q_g, na_k_g, na_rpb, hy_short_w, hy_short_b, hy_w1, hy_b1, hy_w2, hy_b2, hy_w3, hy_freq, hy_d, mla_w_down, mla_qa_g, mla_kva_g, mla_w_uq, mla_w_ukv, mla_q_g, mla_k_g, mla_w_o):
    B, S, D = x.shape
    lc = ctx.shape[1]
    depth = ada_w.shape[0]
    assert B + 1 <= 8
    c8 = jnp.concatenate([c, c_ctx[None, :], jnp.zeros((8 - B - 1, D), F32)], axis=0)
    mods = _ada_modulation(c8, ada_w, ada_b)
    mats_l, mats_c = _dft_mats(S), _dft_mats(lc)
    rope_l = _rope_tables(S)
    rope_c = (jnp.ones((lc, LANES), F32), jnp.zeros((lc, LANES), F32))
    wg, wu, wd = (w.astype(BF16) for w in (moe_w_gate, moe_w_up, moe_w_down))
    for i in range(depth):
        last = i == depth - 1
        j = i // 2
        m_l = [mods[i, :B, None, k * D:(k + 1) * D] for k in range(N_MOD)]
        m_c = [jnp.broadcast_to(mods[i, B, None, None, k * D:(k + 1) * D], (B, 1, D)) for k in range(N_MOD)]
        h_l = _normmod(x, norm1_g[i], m_l[0], m_l[1])
        h_c = _normmod(ctx, norm1_g[i], m_c[0], m_c[1])
        if i % 2 == 0:
            filt = functools.partial(_hyena_filters, w1=hy_w1[j], b1=hy_b1[j], w2=hy_w2[j], b2=hy_b2[j],
                                     w3=hy_w3[j], freq=hy_freq[j])
            y_l, y_c, w_o = _even_mixer(
                h_l, h_c, ev_w_in[j].astype(BF16), ev_w_out[j].astype(BF16), na_q_g[j], na_k_g[j],
                _na_bias(na_rpb[j]), hy_short_w[j], hy_short_b[j], filt(S), filt(lc) if not last else None,
                hy_d[j], mats_l, mats_c, not last)
        else:
            y_l, y_c, w_o = _mla_mixer(h_l, h_c, mla_w_down[j], mla_qa_g[j], mla_kva_g[j], mla_w_uq[j],
                                       mla_w_ukv[j], mla_q_g[j], mla_k_g[j], mla_w_o[j], rope_l, rope_c,
                                       not last)
        x = _gated_residual_mm(y_l, w_o, x, m_l[2], "mixer_out_proj")
        if not last:
            ctx = _gated_residual_mm(y_c, w_o, ctx, m_c[2], "mixer_out_proj")
        x, ctx = _moe(i, x, None if last else ctx, norm2_g[i], m_l, m_c, router_w[i].T, wg, wu, wd)
    return x
```

```python
import functools
import math

import numpy as np
import jax
import jax.numpy as jnp
from jax import lax
from jax.experimental import pallas as pl
from jax.experimental.pallas import tpu as pltpu

F32, BF16, I32, U32 = jnp.float32, jnp.bfloat16, jnp.int32, jnp.uint32
HIGHEST = lax.Precision.HIGHEST

D_MODEL = 2048
DEPTH = 4
GRID_W = 64
N_MOD = 6
NA_HEADS = 8
NA_HEAD_DIM = 128
NA_WIDTH = NA_HEADS * NA_HEAD_DIM
NA_WIN_ROWS = 8
NA_WIN_COLS = 16
HY_WIDTH = D_MODEL - NA_WIDTH
HY_ORDER = 2
HY_EMB = 33
HY_BANDS = (HY_EMB - 1) // 2
HY_HIDDEN = 64
HY_DECAY_SHORT = 0.3
HY_DECAY_LONG = 1.5
HY_DECAY_TARGET = 1e-2
MLA_HEADS = 16
MLA_Q_LORA = 768
MLA_KV_LORA = 512
MLA_NOPE = 128
MLA_ROPE = 64
MLA_V = 128
MLA_QK = MLA_NOPE + MLA_ROPE
MLA_HEAD_PAD = 256
ROPE_BASE = 10000.0
N_EXPERTS = 16
EC_CAPACITY = 2
EXPERT_FF = 1024
EPS = 1e-6
NEG_INF = -1e30

LANES = 128
V7X_VMEM_LIMIT_BYTES = 56 * 1024 * 1024


def _params(n_axes):
    return pltpu.CompilerParams(dimension_semantics=("arbitrary",) * n_axes,
                                vmem_limit_bytes=V7X_VMEM_LIMIT_BYTES)


def _nt(a, b):
    return lax.dot_general(a, b, (((1,), (1,)), ((), ())), preferred_element_type=F32)


def _mm(a, b, *, tm, tn, out_dtype, epi=None, extras=(), n_out=1, order="gij", precision=None, name="mm"):
    a3, b3 = a.ndim == 3, b.ndim == 3
    G = a.shape[0] if a3 else (b.shape[0] if b3 else 1)
    M = a.shape[-2]
    K, N = b.shape[-2:]
    tm, tn = min(tm, M), min(tn, N)
    assert M % tm == 0 and N % tn == 0 and a.shape[-1] >= K
    sizes = dict(g=G, i=M // tm, j=N // tn)
    grid = tuple(sizes[c] for c in order)

    def gij(pid):
        d = dict(zip(order, pid))
        return d["g"], d["i"], d["j"]

    def a_map(*pid):
        g, i, _ = gij(pid)
        return (g, i, 0) if a3 else (i, 0)

    def b_map(*pid):
        g, _, j = gij(pid)
        return (g, 0, j) if b3 else (0, j)

    in_specs = [pl.BlockSpec((None, tm, K) if a3 else (tm, K), a_map),
                pl.BlockSpec((None, K, tn) if b3 else (K, tn), b_map)]
    for _, blk, fn in extras:
        in_specs.append(pl.BlockSpec(blk, lambda *pid, fn=fn: fn(*gij(pid))))

    def body(a_ref, b_ref, *rest):
        extra_refs, o_refs = rest[:len(extras)], rest[len(extras):]
        acc = jnp.dot(a_ref[...], b_ref[...], preferred_element_type=F32, precision=precision)
        outs = acc if epi is None else epi(acc, *extra_refs)
        for o_ref, o in zip(o_refs, outs if n_out > 1 else (outs,)):
            o_ref[...] = o.astype(o_ref.dtype)

    out = pl.pallas_call(
        body, grid=grid, in_specs=in_specs,
        out_specs=[pl.BlockSpec((None, tm, tn), lambda *pid: gij(pid))] * n_out,
        out_shape=[jax.ShapeDtypeStruct((G, M, N), out_dtype)] * n_out,
        compiler_params=_params(3), name=name,
    )(a, b, *[e[0] for e in extras])
    return out if n_out > 1 else out[0]


def _gated_residual_mm(a, w, res, gate, name):
    tm, tn = min(1024, a.shape[1]), min(1024, w.shape[1])
    return _mm(a, w, tm=tm, tn=tn, out_dtype=F32, name=name,
               epi=lambda acc, r_ref, g_ref: r_ref[...] + g_ref[...] * acc,
               extras=[(res, (None, tm, tn), lambda g, i, j: (g, i, j)),
                       (gate, (None, 1, tn), lambda g, i, j: (g, 0, j))])


def _ada_kernel(c_ref, w_ref, b_ref, o_ref):
    c = c_ref[...]
    sc = c * jax.nn.sigmoid(c)
    o_ref[...] = jnp.dot(sc, w_ref[...], preferred_element_type=F32, precision=HIGHEST) + b_ref[...]


def _ada_modulation(c8, ada_w, ada_b):
    depth, d, n = ada_w.shape
    tn = 1024
    return pl.pallas_call(
        _ada_kernel, grid=(depth, n // tn),
        in_specs=[pl.BlockSpec((8, d), lambda l, j: (0, 0)),
                  pl.BlockSpec((None, d, tn), lambda l, j: (l, 0, j)),
                  pl.BlockSpec((None, 1, tn), lambda l, j: (l, 0, j))],
        out_specs=pl.BlockSpec((None, 8, tn), lambda l, j: (l, 0, j)),
        out_shape=jax.ShapeDtypeStruct((depth, 8, n), F32),
        compiler_params=_params(2), name="ada_modulation",
    )(c8, ada_w, ada_b.reshape(depth, 1, n))


def _normmod_kernel(x_ref, g_ref, sh_ref, sc_ref, *rest, with_router):
    x = x_ref[...]
    y = x * lax.rsqrt(jnp.mean(x * x, axis=-1, keepdims=True) + EPS) * g_ref[...]
    h = y * (1.0 + sc_ref[...]) + sh_ref[...]
    if with_router:
        rw_ref, o_ref, lg_ref = rest
        lg_ref[...] = lax.dot_general(rw_ref[...], h, (((1,), (1,)), ((), ())),
                                      preferred_element_type=F32, precision=HIGHEST)
        o_ref[...] = _pack_bf16_pairs(h)
    else:
        (o_ref,) = rest
        o_ref[...] = h.astype(o_ref.dtype)


def _pack_bf16_pairs(h):
    half = h.shape[1] // 2
    bits = pltpu.bitcast(h.astype(BF16).astype(F32), U32)
    return (bits[:, :half] >> 16) | (bits[:, half:] & jnp.uint32(0xFFFF0000))


def _unpack_bf16_pairs(p):
    lo = pltpu.bitcast(p << 16, F32)
    hi = pltpu.bitcast(p & jnp.uint32(0xFFFF0000), F32)
    return jnp.concatenate([lo, hi], axis=1).astype(BF16)


def _normmod(x, g, shift, scale, router_wt=None):
    G, M, D = x.shape
    tm = min(512, M)
    with_router = router_wt is not None
    in_specs = [pl.BlockSpec((None, tm, D), lambda g_, i: (g_, i, 0)),
                pl.BlockSpec((1, D), lambda g_, i: (0, 0)),
                pl.BlockSpec((None, 1, D), lambda g_, i: (g_, 0, 0)),
                pl.BlockSpec((None, 1, D), lambda g_, i: (g_, 0, 0))]
    out_specs = [pl.BlockSpec((None, tm, D), lambda g_, i: (g_, i, 0))]
    out_shape = [jax.ShapeDtypeStruct((G, M, D), BF16)]
    args = [x, g.reshape(1, D), shift, scale]
    if with_router:
        E = router_wt.shape[0]
        out_specs = [pl.BlockSpec((None, tm, D // 2), lambda g_, i: (g_, i, 0))]
        out_shape = [jax.ShapeDtypeStruct((G, M, D // 2), U32)]
        in_specs.append(pl.BlockSpec((E, D), lambda g_, i: (0, 0)))
        out_specs.append(pl.BlockSpec((None, E, tm), lambda g_, i: (g_, 0, i)))
        out_shape.append(jax.ShapeDtypeStruct((G, E, M), F32))
        args.append(router_wt)
    out = pl.pallas_call(
        functools.partial(_normmod_kernel, with_router=with_router), grid=(G, M // tm),
        in_specs=in_specs, out_specs=out_specs, out_shape=out_shape,
        compiler_params=_params(2), name="normmod_router" if with_router else "normmod",
    )(*args)
    return out if with_router else out[0]


def _headnorm_kernel(x_ref, g_ref, o_ref, *, n_heads, head_dim, scale):
    for h in range(n_heads):
        sl = slice(h * head_dim, (h + 1) * head_dim)
        x = x_ref[:, sl].astype(F32)
        y = x * lax.rsqrt(jnp.mean(x * x, axis=-1, keepdims=True) + EPS) * g_ref[...]
        o_ref[:, sl] = (y * scale).astype(o_ref.dtype)


def _headnorm(x, lane_block, g, scale):
    G, M, _ = x.shape
    tm = min(512, M)
    return pl.pallas_call(
        functools.partial(_headnorm_kernel, n_heads=NA_HEADS, head_dim=NA_HEAD_DIM, scale=scale),
        grid=(G, M // tm),
        in_specs=[pl.BlockSpec((None, tm, NA_WIDTH), lambda g_, i: (g_, i, lane_block)),
                  pl.BlockSpec((1, NA_HEAD_DIM), lambda g_, i: (0, 0))],
        out_specs=pl.BlockSpec((None, tm, NA_WIDTH), lambda g_, i: (g_, i, 0)),
        out_shape=jax.ShapeDtypeStruct((G, M, NA_WIDTH), BF16),
        compiler_params=_params(2), name="headnorm",
    )(x, g.reshape(1, NA_HEAD_DIM))


def _attn_kernel(q_ref, k_ref, v_ref, o_ref, *, base2):
    s = _nt(q_ref[...], k_ref[...])
    d = s - jnp.max(s, axis=-1, keepdims=True)
    p = jnp.exp2(d) if base2 else jnp.exp(d)
    l = jnp.sum(p, axis=-1, keepdims=True)
    o = jnp.dot(p.astype(BF16), v_ref[...], preferred_element_type=F32)
    o_ref[...] = (o / l).astype(o_ref.dtype)


def _attention(q, k, v, *, n_heads, dq, dv, v_block, base2=False):
    G, L, _ = q.shape
    return pl.pallas_call(
        functools.partial(_attn_kernel, base2=base2), grid=(G, n_heads),
        in_specs=[pl.BlockSpec((None, L, dq), lambda g, h: (g, 0, h)),
                  pl.BlockSpec((None, L, dq), lambda g, h: (g, 0, h)),
                  pl.BlockSpec((None, L, dv), lambda g, h: (g, 0, v_block(h)))],
        out_specs=pl.BlockSpec((None, L, dv), lambda g, h: (g, 0, h)),
        out_shape=jax.ShapeDtypeStruct((G, L, n_heads * dv), BF16),
        compiler_params=_params(2), name="ctx_attention",
    )(q, k, v)


MLA_KEY_BLOCK = 512


def _mla_attn_kernel(q_ref, kc_ref, vc_ref, kl_ref, vl_ref, o_ref):
    q = q_ref[...]

    def probs(s, m):
        return jnp.exp2((s - m).astype(BF16))

    s = _nt(q, kc_ref[...])
    m = jnp.max(s, axis=-1, keepdims=True)
    acc = jnp.dot(probs(s, m), vc_ref[...], preferred_element_type=F32)
    for j in range(kl_ref.shape[0] // MLA_KEY_BLOCK):
        rows = slice(j * MLA_KEY_BLOCK, (j + 1) * MLA_KEY_BLOCK)
        s = _nt(q, kl_ref[rows, :])
        m_new = jnp.maximum(m, jnp.max(s, axis=-1, keepdims=True))
        acc = jnp.exp2(m - m_new) * acc + jnp.dot(probs(s, m_new), vl_ref[rows, :], preferred_element_type=F32)
        m = m_new
    o_ref[...] = (acc[:, :MLA_V] / acc[:, MLA_V:]).astype(o_ref.dtype)


def _mla_attention(q, k_c, v_c, k_l, v_l):
    G, S, _ = q.shape
    lc = k_c.shape[1]
    tq = min(512, S)
    assert S % MLA_KEY_BLOCK == 0
    dq = MLA_HEAD_PAD
    head = lambda g, h, i: (g, 0, h)
    return pl.pallas_call(
        _mla_attn_kernel, grid=(G, MLA_HEADS, S // tq),
        in_specs=[pl.BlockSpec((None, tq, dq), lambda g, h, i: (g, i, h)),
                  pl.BlockSpec((None, lc, dq), head), pl.BlockSpec((None, lc, dq), head),
                  pl.BlockSpec((None, S, dq), head), pl.BlockSpec((None, S, dq), head)],
        out_specs=pl.BlockSpec((None, tq, MLA_V), lambda g, h, i: (g, i, h)),
        out_shape=jax.ShapeDtypeStruct((G, S, MLA_HEADS * MLA_V), BF16),
        compiler_params=_params(3), name="mla_attention",
    )(q, k_c, v_c, k_l, v_l)


NA_ROWS_PER_STEP = 8


def _na_kernel(q_ref, k_ref, v_ref, kc_ref, vc_ref, bias_ref, o_ref, *, n_rows):
    rb = pl.program_id(2)
    kc = kc_ref[...]
    vc = vc_ref[...]
    win = NA_WIN_ROWS * GRID_W
    q = q_ref[...]
    starts, s_w = [], []
    for i in range(NA_ROWS_PER_STEP):
        r = rb * NA_ROWS_PER_STEP + i
        r0 = jnp.clip(r - NA_WIN_ROWS // 2, 0, n_rows - NA_WIN_ROWS)
        starts.append(pl.multiple_of(r0 * GRID_W, GRID_W))
        s_w.append(_nt(q[i * GRID_W:(i + 1) * GRID_W, :], k_ref[pl.ds(starts[i], win), :]) + bias_ref[r - r0])
    s_w = jnp.concatenate(s_w, axis=0)
    s_c = _nt(q, kc)
    m = jnp.maximum(jnp.max(s_w, axis=-1, keepdims=True), jnp.max(s_c, axis=-1, keepdims=True))
    p_w = jnp.exp(s_w - m)
    p_c = jnp.exp(s_c - m)
    l = jnp.sum(p_w, axis=-1, keepdims=True) + jnp.sum(p_c, axis=-1, keepdims=True)
    p_w = p_w.astype(BF16)
    o_w = [jnp.dot(p_w[i * GRID_W:(i + 1) * GRID_W, :], v_ref[pl.ds(starts[i], win), :],
                   preferred_element_type=F32) for i in range(NA_ROWS_PER_STEP)]
    o = jnp.concatenate(o_w, axis=0) + jnp.dot(p_c.astype(BF16), vc, preferred_element_type=F32)
    o_ref[...] = (o / l).astype(o_ref.dtype)


def _na_attention(q, k, proj, kc, proj_c, bias):
    G, S, _ = q.shape
    lc = kc.shape[1]
    n_rows = S // GRID_W
    assert n_rows >= NA_WIN_ROWS and n_rows % NA_ROWS_PER_STEP == 0
    tq = NA_ROWS_PER_STEP * GRID_W
    hd = NA_HEAD_DIM
    v_off = 2 * NA_HEADS
    return pl.pallas_call(
        functools.partial(_na_kernel, n_rows=n_rows),
        grid=(G, NA_HEADS, n_rows // NA_ROWS_PER_STEP),
        in_specs=[pl.BlockSpec((None, tq, hd), lambda g, h, r: (g, r, h)),
                  pl.BlockSpec((None, S, hd), lambda g, h, r: (g, 0, h)),
                  pl.BlockSpec((None, S, hd), lambda g, h, r: (g, 0, v_off + h)),
                  pl.BlockSpec((None, lc, hd), lambda g, h, r: (g, 0, h)),
                  pl.BlockSpec((None, lc, hd), lambda g, h, r: (g, 0, v_off + h)),
                  pl.BlockSpec((None, NA_WIN_ROWS, GRID_W, NA_WIN_ROWS * GRID_W),
                               lambda g, h, r: (h, 0, 0, 0))],
        out_specs=pl.BlockSpec((None, tq, hd), lambda g, h, r: (g, r, h)),
        out_shape=jax.ShapeDtypeStruct((G, S, NA_WIDTH), BF16),
        compiler_params=_params(3), name="na_attention",
    )(q, k, proj, kc, proj_c, bias)


def _na_bias(rpb):
    cols = np.arange(GRID_W)
    col_start = np.clip(cols - NA_WIN_COLS // 2, 0, GRID_W - NA_WIN_COLS)
    in_win = (cols[None, :] >= col_start[:, None]) & (cols[None, :] < col_start[:, None] + NA_WIN_COLS)
    wr, wc = NA_WIN_ROWS, NA_WIN_COLS
    a = jnp.stack([rpb[:, wr - 1 - v:2 * wr - 1 - v, :] for v in range(wr)], axis=1).astype(F32)
    pad = GRID_W - wc
    ap = jnp.pad(a, ((0, 0), (0, 0), (0, 0), (pad, pad)))
    b = jnp.stack([ap[..., GRID_W - 1 - q:2 * GRID_W - 1 - q] for q in range(GRID_W)], axis=3)
    b = jnp.transpose(b, (0, 1, 3, 2, 4))
    b = jnp.where(in_win[None, None, :, None, :], b, NEG_INF)
    return b.reshape(rpb.shape[0], wr, GRID_W, wr * GRID_W)


def _shortconv_kernel(x_ref, w_ref, b_ref, o_ref):
    x = x_ref[...].astype(F32)
    n = x.shape[0]
    row = lax.broadcasted_iota(I32, x.shape, 0)
    prev = jnp.where(row == 0, 0.0, pltpu.roll(x, 1, 0))
    nxt = jnp.where(row == n - 1, 0.0, pltpu.roll(x, n - 1, 0))
    w = w_ref[...]
    o_ref[...] = (w[0:1] * prev + w[1:2] * x + w[2:3] * nxt + b_ref[...]).astype(o_ref.dtype)


def _shortconv(proj, lane_off, w, b):
    G, L, _ = proj.shape
    C = w.shape[1]
    tc = 256
    off = lane_off // tc
    return pl.pallas_call(
        _shortconv_kernel, grid=(G, C // tc),
        in_specs=[pl.BlockSpec((None, L, tc), lambda g, j: (g, 0, off + j)),
                  pl.BlockSpec((3, tc), lambda g, j: (0, j)),
                  pl.BlockSpec((1, tc), lambda g, j: (0, j))],
        out_specs=pl.BlockSpec((None, L, tc), lambda g, j: (g, 0, j)),
        out_shape=jax.ShapeDtypeStruct((G, L, C), BF16),
        compiler_params=_params(2), name="hyena_shortconv",
    )(proj, w, b.reshape(1, C))


def _hy_blocks(L):
    return 4 if L >= 1024 else 1


def _hyfilt_kernel(z_ref, zr_ref, w1_ref, b1_ref, w2_ref, b2_ref, fr_ref, w3f_ref, w3b_ref, dl_ref,
                   os_ref, od_ref, hid_s, hidr_s, *, n_blocks):
    @pl.when((pl.program_id(0) == 0) & (pl.program_id(1) == 0))
    def _():
        fr = fr_ref[...]
        for src, dst in ((z_ref, hid_s), (zr_ref, hidr_s)):
            hid = jnp.sin(fr * (jnp.dot(src[...], w1_ref[...], preferred_element_type=F32, precision=HIGHEST)
                                + b1_ref[...]))
            dst[...] = jnp.sin(fr * (jnp.dot(hid, w2_ref[...], preferred_element_type=F32, precision=HIGHEST)
                                     + b2_ref[...])).astype(dst.dtype)

    def filt(w3_ref, pos_ref, hid_ref):
        decay = jnp.exp(-pos_ref[:, 0:1] * dl_ref[...])
        f = jnp.dot(hid_ref[...], w3_ref[...].astype(BF16), preferred_element_type=F32) * decay
        return f * lax.rsqrt(jnp.sum(f * f, axis=0, keepdims=True) + EPS)

    L = z_ref.shape[0]
    P = L // n_blocks
    hf, hb = filt(w3f_ref, z_ref, hid_s), filt(w3b_ref, z_ref, hid_s)
    if n_blocks > 1:
        hf_rev, hb_rev = filt(w3f_ref, zr_ref, hidr_s), filt(w3b_ref, zr_ref, hidr_s)
    row0 = lax.broadcasted_iota(I32, (P, hf.shape[1]), 0) == 0
    for jj in range(2 * n_blocks - 1):
        j = jj - (n_blocks - 1)
        a = hf[j * P:(j + 1) * P] if j >= 0 else hb_rev[L + j * P:L + (j + 1) * P]
        b = hf_rev[L - j * P:L - (j - 1) * P] if j >= 1 else hb[-j * P:(1 - j) * P]
        if j == 0:
            a = a + jnp.where(row0, b, 0.0)
        b = jnp.where(row0, 0.0, b)
        os_ref[jj] = (a + b).astype(os_ref.dtype)
        od_ref[jj] = (a - b).astype(od_ref.dtype)


def _hyena_filters(L, w1, b1, w2, b2, w3, freq):
    t = jnp.linspace(0.0, 1.0, L, dtype=F32)[:, None]
    wv = 2.0 * math.pi * jnp.arange(L, dtype=F32)[:, None] / L
    bands = jnp.linspace(1e-4, HY_BANDS - 1, HY_BANDS, dtype=F32)[None, :]
    z = jnp.concatenate([t, jnp.cos(bands * wv), -jnp.sin(bands * wv)], axis=-1)
    z = jnp.pad(z, ((0, 0), (0, LANES - HY_EMB)))
    z_rev = jnp.roll(z[::-1], 1, axis=0)
    hp = LANES - HY_HIDDEN
    w1p = jnp.pad(w1, ((0, LANES - HY_EMB), (0, hp)))
    w2p = jnp.pad(w2, ((0, hp), (0, hp)))
    w3p = jnp.pad(w3, ((0, hp), (0, 0)))
    b1p = jnp.pad(b1, (0, hp)).reshape(1, LANES)
    b2p = jnp.pad(b2, (0, hp)).reshape(1, LANES)
    frp = jnp.pad(freq, (0, hp)).reshape(1, LANES)
    d_lo = math.log(HY_DECAY_TARGET) / HY_DECAY_LONG
    d_hi = math.log(HY_DECAY_TARGET) / HY_DECAY_SHORT
    deltas = jnp.abs(jnp.linspace(d_lo, d_hi, HY_WIDTH, dtype=F32)).reshape(1, HY_WIDTH)
    tc = 128
    nct = HY_WIDTH // tc
    n_blocks = _hy_blocks(L)
    n_lags, P = 2 * n_blocks - 1, L // n_blocks
    const = lambda o, j: (0, 0)
    out = jax.ShapeDtypeStruct((n_lags, P, HY_ORDER * HY_WIDTH), BF16)
    return pl.pallas_call(
        functools.partial(_hyfilt_kernel, n_blocks=n_blocks), grid=(HY_ORDER, nct),
        in_specs=[pl.BlockSpec((L, LANES), const), pl.BlockSpec((L, LANES), const),
                  pl.BlockSpec((LANES, LANES), const),
                  pl.BlockSpec((1, LANES), const), pl.BlockSpec((LANES, LANES), const),
                  pl.BlockSpec((1, LANES), const), pl.BlockSpec((1, LANES), const),
                  pl.BlockSpec((LANES, tc), lambda o, j: (0, 2 * o * nct + j)),
                  pl.BlockSpec((LANES, tc), lambda o, j: (0, (2 * o + 1) * nct + j)),
                  pl.BlockSpec((1, tc), lambda o, j: (0, j))],
        out_specs=[pl.BlockSpec((n_lags, P, tc), lambda o, j: (0, 0, o * nct + j))] * 2,
        out_shape=[out, out], scratch_shapes=[pltpu.VMEM((L, LANES), BF16), pltpu.VMEM((L, LANES), BF16)],
        compiler_params=_params(2), name="hyena_filters",
    )(z, z_rev, w1p, b1p, w2p, b2p, frp, w3p, w3p, deltas)


def _dft_tables(L):
    A = 1 << (int(math.log2(L)) // 2)
    period = 4 * L
    k = np.arange(L, dtype=np.int64)[:, None]
    ang_a = 2.0 * np.pi * (((2 * k + 1) * (A * np.arange(L // A, dtype=np.int64)[None, :])) % period) / period
    ang_b = 2.0 * np.pi * (((2 * k + 1) * np.arange(A, dtype=np.int64)[None, :]) % period) / period
    return tuple(np.asarray(f(x), np.float32) for x in (ang_a, ang_b) for f in (np.cos, np.sin))


def _dft_mats(L):
    ca, sa, cb, sb = (jnp.asarray(t) for t in _dft_tables(L))
    fc = (ca[:, :, None] * cb[:, None, :] - sa[:, :, None] * sb[:, None, :]).reshape(L, L)
    fs = (sa[:, :, None] * cb[:, None, :] + ca[:, :, None] * sb[:, None, :]).reshape(L, L)
    cat, sat, cbt, sbt = ca.T, sa.T, cb.T, sb.T
    fct = (cat[:, None, :] * cbt[None, :, :] - sat[:, None, :] * sbt[None, :, :]).reshape(L, L)
    fst = (sat[:, None, :] * cbt[None, :, :] + cat[:, None, :] * sbt[None, :, :]).reshape(L, L)
    return tuple(m.astype(BF16) for m in (fc, fs, fct, fst))


def _dft_fwd_kernel(fc_ref, fs_ref, u_ref, hc_ref, hs_ref, yc_ref, ys_ref, *, n_blocks):
    P = fc_ref.shape[0]
    fc, fs = fc_ref[...], fs_ref[...]
    xc, xs = [], []
    for i in range(n_blocks):
        u = u_ref[i * P:(i + 1) * P, :]
        xc.append(jnp.dot(fc, u, preferred_element_type=F32))
        xs.append(jnp.dot(fs, u, preferred_element_type=F32))
    for o in range(n_blocks):
        yc = ys = None
        for i in range(n_blocks):
            hc, hs = hc_ref[o - i + n_blocks - 1], hs_ref[o - i + n_blocks - 1]
            tc_, ts_ = xc[i] * hc - xs[i] * hs, xc[i] * hs + xs[i] * hc
            yc, ys = (tc_, ts_) if yc is None else (yc + tc_, ys + ts_)
        yc_ref[o * P:(o + 1) * P, :] = yc.astype(yc_ref.dtype)
        ys_ref[o * P:(o + 1) * P, :] = ys.astype(ys_ref.dtype)


def _dft_fwd(fc, fs, u, u_block, hc, hs, h_block):
    G, L, _ = u.shape
    n_lags, P, _ = hc.shape
    C = HY_WIDTH
    tc = 256
    nct = C // tc
    spec_f = pl.BlockSpec((P, P), lambda j, g: (0, 0))
    spec_h = pl.BlockSpec((n_lags, P, tc), lambda j, g: (0, 0, h_block * nct + j), pipeline_mode=pl.Buffered(1))
    spec_y = pl.BlockSpec((None, L, tc), lambda j, g: (g, 0, j))
    out = jax.ShapeDtypeStruct((G, L, C), BF16)
    return pl.pallas_call(
        functools.partial(_dft_fwd_kernel, n_blocks=(n_lags + 1) // 2), grid=(nct, G),
        in_specs=[spec_f, spec_f, pl.BlockSpec((None, L, tc), lambda j, g: (g, 0, u_block * nct + j)),
                  spec_h, spec_h],
        out_specs=[spec_y, spec_y], out_shape=[out, out],
        compiler_params=_params(2), name="hyena_dft_fwd",
    )(fc, fs, u, hc, hs)


def _dft_inv_kernel(fct_ref, fst_ref, yc_ref, ys_ref, u_ref, x_ref, d_ref, o_ref, *, inv_scale):
    y = (jnp.dot(fct_ref[...], yc_ref[...], preferred_element_type=F32)
         + jnp.dot(fst_ref[...], ys_ref[...], preferred_element_type=F32)) * inv_scale
    y = y + u_ref[...].astype(F32) * d_ref[...]
    o_ref[...] = (x_ref[...].astype(F32) * y).astype(o_ref.dtype)


def _dft_inv(fct, fst, yc, ys, u, u_block, gate, gate_block, d):
    G, L, C = yc.shape
    tr, tc = min(512, L), 512
    nct = C // tc
    spec_f = pl.BlockSpec((tr, L), lambda n, g, j: (n, 0))
    spec_y = pl.BlockSpec((None, L, tc), lambda n, g, j: (g, 0, j))
    return pl.pallas_call(
        functools.partial(_dft_inv_kernel, inv_scale=1.0 / L),
        grid=(L // tr, G, nct),
        in_specs=[spec_f, spec_f, spec_y, spec_y,
                  pl.BlockSpec((None, tr, tc), lambda n, g, j: (g, n, u_block * nct + j)),
                  pl.BlockSpec((None, tr, tc), lambda n, g, j: (g, n, gate_block * nct + j)),
                  pl.BlockSpec((1, tc), lambda n, g, j: (0, j))],
        out_specs=pl.BlockSpec((None, tr, tc), lambda n, g, j: (g, n, j)),
        out_shape=jax.ShapeDtypeStruct((G, L, C), BF16),
        compiler_params=_params(3), name="hyena_dft_inv",
    )(fct, fst, yc, ys, u, gate, d.reshape(1, C))


def _hyena(proj, short_w, short_b, hsum, hdiff, hy_d, mats):
    fc, fs, fct, fst = mats
    G, L, _ = proj.shape
    n_blocks = (hsum.shape[0] + 1) // 2
    P = L // n_blocks
    blocks = lambda t: t.reshape(G * n_blocks, P, t.shape[-1])
    sc = _shortconv(proj, 3 * NA_WIDTH, short_w, short_b)
    hc = _mm(fc, hsum, tm=512, tn=512, out_dtype=F32, name="hyena_filter_dft")
    hs = _mm(fs, hdiff, tm=512, tn=512, out_dtype=F32, name="hyena_filter_dft")
    z = sc
    for o in range(HY_ORDER):
        yc, ys = _dft_fwd(fc, fs, z, 0, hc, hs, o)
        z = _dft_inv(fct, fst, blocks(yc), blocks(ys), blocks(z), 0, blocks(sc), o + 1, hy_d[o])
        z = z.reshape(G, L, HY_WIDTH)
    return z


def _rope_rotate(y):
    lane = lax.broadcasted_iota(I32, y.shape, 1)
    first = (lane % 32) < 16
    return jnp.where(first, -pltpu.roll(y, LANES - 16, 1), pltpu.roll(y, 16, 1))


def _mla_q_epilogue(acc, g_ref, cos_ref, sin_ref, *, scale):
    cos, sin = cos_ref[...], sin_ref[...]
    g = g_ref[...]
    pieces = []
    for h in range(acc.shape[1] // MLA_HEAD_PAD):
        lo = h * MLA_HEAD_PAD
        xn = acc[:, lo:lo + LANES]
        xr = acc[:, lo + LANES:lo + 2 * LANES]
        ss = jnp.sum(xn * xn + xr * xr, axis=-1, keepdims=True)
        rs = lax.rsqrt(ss * (1.0 / MLA_QK) + EPS) * scale
        yr = xr * rs * g[:, LANES:]
        pieces += [xn * rs * g[:, :LANES], yr * cos + _rope_rotate(yr) * sin]
    return jnp.concatenate(pieces, axis=1)


def _mla_kv_epilogue(acc, kr_ref, g_ref, cos_ref, sin_ref):
    cos, sin = cos_ref[...], sin_ref[...]
    g = g_ref[...]
    kr = kr_ref[...].astype(F32)
    sr = jnp.sum(kr * kr, axis=-1, keepdims=True)
    krg = kr * g[:, LANES:]
    krot = krg * cos + _rope_rotate(krg) * sin
    ones = jnp.ones((acc.shape[0], LANES), F32)
    ks, vs = [], []
    for h in range(acc.shape[1] // MLA_HEAD_PAD):
        lo = h * MLA_HEAD_PAD
        kn = acc[:, lo:lo + LANES]
        rs = lax.rsqrt((jnp.sum(kn * kn, axis=-1, keepdims=True) + sr) * (1.0 / MLA_QK) + EPS)
        ks += [kn * rs * g[:, :LANES], krot * rs]
        vs += [acc[:, lo + LANES:lo + 2 * LANES], ones]
    return jnp.concatenate(ks, axis=1), jnp.concatenate(vs, axis=1)


def _rope_tables(S):
    t = jnp.arange(S)
    half = MLA_ROPE // 2
    inv = ROPE_BASE ** (-jnp.arange(0, half, 2, dtype=F32) / half)
    ang_r = (t // GRID_W).astype(F32)[:, None] * inv[None, :]
    ang_c = (t % GRID_W).astype(F32)[:, None] * inv[None, :]
    ang = jnp.concatenate([ang_r, ang_r, ang_c, ang_c], axis=-1)
    pad = LANES - MLA_ROPE
    return (jnp.pad(jnp.cos(ang), ((0, 0), (0, pad)), constant_values=1.0),
            jnp.pad(jnp.sin(ang), ((0, 0), (0, pad))))


def _lane_cumsum(x):
    n = x.shape[1]
    tri = (lax.broadcasted_iota(I32, (LANES, LANES), 0) <= lax.broadcasted_iota(I32, (LANES, LANES), 1))
    tri = jnp.where(tri, 1.0, 0.0).astype(BF16)
    off = jnp.zeros((x.shape[0], 1), F32)
    outs = []
    for j in range(n // LANES):
        c = jnp.dot(x[:, j * LANES:(j + 1) * LANES], tri, preferred_element_type=F32) + off
        outs.append(c)
        off = c[:, LANES - 1:LANES]
    return jnp.concatenate(outs, axis=1)


ROUTE_SLOT_CHUNK = 64


def _route_kernel(lg_ref, idx_ref, gt_ref, aff_s, slot_s, *, cap, chunk):
    lg = lg_ref[...]
    n_exp, n_tok = lg.shape
    ex = jnp.exp(lg - jnp.max(lg, axis=0, keepdims=True))
    aff = ex / jnp.sum(ex, axis=0, keepdims=True)
    bits = pltpu.bitcast(aff, I32)

    def search(i, v):
        cand = v | jnp.left_shift(jnp.int32(1), 30 - i)
        cnt = jnp.sum(jnp.where(bits >= cand, 1.0, 0.0), axis=1, keepdims=True)
        return jnp.where(cnt >= cap, cand, v)

    thr = lax.fori_loop(0, 31, search, jnp.zeros((n_exp, 1), I32))
    gt = bits > thr
    eq = bits == thr
    need = cap - jnp.sum(jnp.where(gt, 1.0, 0.0), axis=1, keepdims=True)
    eq_rank = _lane_cumsum(jnp.where(eq, 1.0, 0.0).astype(BF16))
    sel = jnp.where(gt, 1.0, jnp.where(eq, jnp.where(eq_rank <= need, 1.0, 0.0), 0.0))
    slot = _lane_cumsum(sel.astype(BF16)) * sel
    aff_s[...] = aff
    slot_s[...] = slot
    tok = lax.broadcasted_iota(I32, (chunk, n_tok), 1).astype(F32)

    def per_expert(e, carry):
        srow = slot_s[pl.ds(e, 1), :]
        arow = aff_s[pl.ds(e, 1), :]

        def per_chunk(c, carry2):
            p0 = pl.multiple_of(c * chunk, chunk)
            want = (lax.broadcasted_iota(I32, (chunk, 1), 0) + (p0 + 1)).astype(F32)
            hit = srow == want
            idx_ref[e, pl.ds(p0, chunk), :] = jnp.sum(
                jnp.where(hit, tok, 0.0), axis=1, keepdims=True).astype(I32)
            gt_ref[e, pl.ds(p0, chunk), :] = jnp.sum(
                jnp.where(hit, arow, 0.0), axis=1, keepdims=True)
            return carry2

        return lax.fori_loop(0, cap // chunk, per_chunk, carry)

    lax.fori_loop(0, n_exp, per_expert, 0)


def _route(logits_t, cap):
    G, E, N = logits_t.shape
    chunk = min(ROUTE_SLOT_CHUNK, cap)
    assert cap % chunk == 0 and N % LANES == 0
    kern = functools.partial(_route_kernel, cap=cap, chunk=chunk)
    spec_o = pl.BlockSpec((None, E, cap, 1), lambda g: (g, 0, 0, 0))
    return pl.pallas_call(
        kern, grid=(G,), in_specs=[pl.BlockSpec((None, E, N), lambda g: (g, 0, 0))],
        out_specs=[spec_o, spec_o],
        out_shape=[jax.ShapeDtypeStruct((G, E, cap, 1), I32), jax.ShapeDtypeStruct((G, E, cap, 1), F32)],
        scratch_shapes=[pltpu.VMEM((E, N), F32), pltpu.VMEM((E, N), F32)],
        compiler_params=_params(1), name="moe_route",
    )(logits_t)


def _gather_kernel(idx_ref, h_ref, o_ref, *, cap, n_exp):
    base = (pl.program_id(0) * n_exp + pl.program_id(1)) * cap

    def body(p, carry):
        r = idx_ref[base + p]
        o_ref[pl.ds(p, 1), :] = h_ref[pl.ds(r, 1), :]
        return carry

    lax.fori_loop(0, cap, body, 0, unroll=8)


def _gather(idx_flat, h32, n_exp, cap):
    G, N, W = h32.shape
    return pl.pallas_call(
        functools.partial(_gather_kernel, cap=cap, n_exp=n_exp),
        grid_spec=pltpu.PrefetchScalarGridSpec(
            num_scalar_prefetch=1, grid=(G, n_exp),
            in_specs=[pl.BlockSpec((None, N, W), lambda g, e, idx: (g, 0, 0))],
            out_specs=pl.BlockSpec((None, None, cap, W), lambda g, e, idx: (g, e, 0, 0))),
        out_shape=jax.ShapeDtypeStruct((G, n_exp, cap, W), h32.dtype),
        compiler_params=_params(2), name="moe_gather",
    )(idx_flat, h32)


def _swiglu(x32, slot_gate, feat_gate, wg_ref, wu_ref, wd_ref):
    x = _unpack_bf16_pairs(x32)
    a = jnp.dot(x, wg_ref[...], preferred_element_type=F32)
    u = jnp.dot(x, wu_ref[...], preferred_element_type=F32)
    h = (a * jax.nn.sigmoid(a) * u).astype(BF16)
    return jnp.dot(h, wd_ref[...], preferred_element_type=F32) * slot_gate * feat_gate


def _ffn_kernel(*refs, n_lat, with_ctx):
    if with_ctx:
        xl_ref, gl_ref, fl_ref, xc_ref, gc_ref, fc_ref, wg_ref, wu_ref, wd_ref, yl_ref, yc_ref = refs
    else:
        xl_ref, gl_ref, fl_ref, wg_ref, wu_ref, wd_ref, yl_ref = refs
    step = pl.program_id(1)

    @pl.when(step < n_lat)
    def _():
        yl_ref[...] = _swiglu(xl_ref[...], gl_ref[...], fl_ref[...], wg_ref, wu_ref, wd_ref)

    if with_ctx:
        @pl.when(step == n_lat)
        def _():
            gb, cc, w = xc_ref.shape
            y = _swiglu(xc_ref[...].reshape(gb * cc, w), gc_ref[...].reshape(gb * cc, 1), fc_ref[...],
                        wg_ref, wu_ref, wd_ref)
            yc_ref[...] = y.reshape(gb, cc, y.shape[1])


def _expert_ffn(layer, wg, wu, wd, lat, ctx=None):
    xl, gl, fl = lat
    G, E, C, W = xl.shape
    D, FF = wg.shape[-2:]
    lat_slot = lambda e, s: (jnp.minimum(s, G - 1), e, 0, 0)
    w_idx = lambda e, s: (layer, e, 0, 0)
    in_specs = [pl.BlockSpec((None, None, C, W), lat_slot), pl.BlockSpec((None, None, C, 1), lat_slot),
                pl.BlockSpec((None, 1, D), lambda e, s: (jnp.minimum(s, G - 1), 0, 0))]
    out_specs = [pl.BlockSpec((None, None, C, D), lat_slot)]
    out_shape = [jax.ShapeDtypeStruct((G, E, C, D), F32)]
    args = [xl, gl, fl]
    if ctx is not None:
        xc, gc, fc = ctx
        Gc, _, Cc, _ = xc.shape
        ctx_slot = lambda e, s: (0, e, 0, 0)
        in_specs += [pl.BlockSpec((Gc, None, Cc, W), ctx_slot), pl.BlockSpec((Gc, None, Cc, 1), ctx_slot),
                     pl.BlockSpec((1, D), lambda e, s: (0, 0))]
        out_specs.append(pl.BlockSpec((Gc, None, Cc, D), ctx_slot))
        out_shape.append(jax.ShapeDtypeStruct((Gc, E, Cc, D), F32))
        args += [xc, gc, fc]
    in_specs += [pl.BlockSpec((None, None, D, FF), w_idx), pl.BlockSpec((None, None, D, FF), w_idx),
                 pl.BlockSpec((None, None, FF, D), w_idx)]
    return pl.pallas_call(
        functools.partial(_ffn_kernel, n_lat=G, with_ctx=ctx is not None),
        grid=(E, G + (ctx is not None)), in_specs=in_specs, out_specs=out_specs, out_shape=out_shape,
        compiler_params=_params(2), name="moe_expert_ffn",
    )(*args, wg, wu, wd)


COMBINE_ROWS = 8


def _combine_kernel(idx_ref, y_ref, x_ref, o_ref, *, cap, n_exp):
    e = pl.program_id(2)
    base = (pl.program_id(0) * n_exp + e) * cap

    @pl.when(e == 0)
    def _():
        o_ref[...] = x_ref[...]

    def body(i, carry):
        p0 = pl.multiple_of(i * COMBINE_ROWS, COMBINE_ROWS)
        rows = [idx_ref[base + p0 + k] for k in range(COMBINE_ROWS)]
        vals = [o_ref[pl.ds(rows[k], 1), :] + y_ref[pl.ds(p0 + k, 1), :] for k in range(COMBINE_ROWS)]
        for k in range(COMBINE_ROWS):
            o_ref[pl.ds(rows[k], 1), :] = vals[k]
        return carry

    lax.fori_loop(0, cap // COMBINE_ROWS, body, 0)


def _combine(idx_flat, y, x):
    G, E, C, D = y.shape
    N = x.shape[1]
    tc = 1024
    assert C % COMBINE_ROWS == 0
    return pl.pallas_call(
        functools.partial(_combine_kernel, cap=C, n_exp=E),
        grid_spec=pltpu.PrefetchScalarGridSpec(
            num_scalar_prefetch=1, grid=(G, D // tc, E),
            in_specs=[pl.BlockSpec((None, None, C, tc), lambda g, j, e, idx: (g, e, 0, j)),
                      pl.BlockSpec((None, N, tc), lambda g, j, e, idx: (g, 0, j),
                                   pipeline_mode=pl.Buffered(1))],
            out_specs=pl.BlockSpec((None, N, tc), lambda g, j, e, idx: (g, 0, j))),
        out_shape=jax.ShapeDtypeStruct((G, N, D), F32),
        compiler_params=_params(3), name="moe_combine",
    )(idx_flat, y, x)


def _route_and_gather(x, g2, shift, scale, router_wt):
    G, N, _ = x.shape
    E = router_wt.shape[0]
    cap = max(1, (EC_CAPACITY * N) // E)
    h32, logits_t = _normmod(x, g2, shift, scale, router_wt)
    idx, gates = _route(logits_t, cap)
    idx_flat = idx.reshape(G * E * cap)
    return idx_flat, gates, _gather(idx_flat, h32, E, cap)


def _moe(layer, x, ctx, g2, m_l, m_c, router_wt, wg, wu, wd):
    il, gl, xl = _route_and_gather(x, g2, m_l[3], m_l[4], router_wt)
    if ctx is None:
        (yl,) = _expert_ffn(layer, wg, wu, wd, (xl, gl, m_l[5]))
        return _combine(il, yl, x), None
    ic, gc, xc = _route_and_gather(ctx, g2, m_c[3], m_c[4], router_wt)
    yl, yc = _expert_ffn(layer, wg, wu, wd, (xl, gl, m_l[5]), (xc, gc, m_c[5][0]))
    return _combine(il, yl, x), _combine(ic, yc, ctx)


def _even_mixer(h_l, h_c, w_in, w_out, q_g, k_g, bias, short_w, short_b, filt_l, filt_c, hy_d,
                mats_l, mats_c, need_ctx):
    scale = NA_HEAD_DIM ** -0.5
    proj_l = _mm(h_l, w_in, tm=1024, tn=1024, out_dtype=BF16, name="even_in_proj")
    proj_c = _mm(h_c, w_in, tm=1024, tn=1024, out_dtype=BF16, name="even_in_proj")
    q_l = _headnorm(proj_l, 0, q_g, scale)
    k_l = _headnorm(proj_l, 1, k_g, 1.0)
    k_c = _headnorm(proj_c, 1, k_g, 1.0)
    na_l = _na_attention(q_l, k_l, proj_l, k_c, proj_c, bias)
    hy_l = _hyena(proj_l, short_w, short_b, *filt_l, hy_d, mats_l)
    y_l = jnp.concatenate([na_l, hy_l], axis=-1)
    y_c = None
    if need_ctx:
        q_c = _headnorm(proj_c, 0, q_g, scale)
        na_c = _attention(q_c, k_c, proj_c, n_heads=NA_HEADS, dq=NA_HEAD_DIM, dv=NA_HEAD_DIM,
                          v_block=lambda h: 2 * NA_HEADS + h)
        hy_c = _hyena(proj_c, short_w, short_b, *filt_c, hy_d, mats_c)
        y_c = jnp.concatenate([na_c, hy_c], axis=-1)
    return y_l, y_c, w_out


def _rms_rows(acc, g_ref, width):
    x = acc[:, :width]
    y = x * lax.rsqrt(jnp.mean(x * x, axis=-1, keepdims=True) + EPS) * g_ref[...]
    return y if width == acc.shape[1] else jnp.concatenate([y, acc[:, width:]], axis=1)


def _mla_project(h, w_dq, w_dkv, qa_g, kva_g, w_uq, w_ukv, q_g, k_g, cos, sin, with_q):
    scale = MLA_QK ** -0.5 * math.log2(math.e)
    kvw = w_dkv.shape[1]
    tm = min(1024, h.shape[1])
    const = lambda g, i, j: (0, 0)
    rope = [(cos, (tm, LANES), lambda g, i, j: (i, 0)), (sin, (tm, LANES), lambda g, i, j: (i, 0))]
    dkv = _mm(h, w_dkv, tm=tm, tn=kvw, out_dtype=BF16, name="mla_down_kv",
              epi=lambda acc, g_ref: _rms_rows(acc, g_ref, MLA_KV_LORA),
              extras=[(kva_g.reshape(1, -1), (1, MLA_KV_LORA), const)])
    kr_block = MLA_KV_LORA // LANES
    k, v = _mm(dkv, w_ukv, tm=tm, tn=1024, out_dtype=BF16, n_out=2, name="mla_up_kv", epi=_mla_kv_epilogue,
               extras=[(dkv, (None, tm, LANES), lambda g, i, j: (g, i, kr_block)),
                       (k_g, (1, MLA_HEAD_PAD), const)] + rope)
    q = None
    if with_q:
        qa = _mm(h, w_dq, tm=tm, tn=MLA_Q_LORA, out_dtype=BF16, name="mla_down_q",
                 epi=lambda acc, g_ref: _rms_rows(acc, g_ref, MLA_Q_LORA),
                 extras=[(qa_g.reshape(1, -1), (1, MLA_Q_LORA), const)])
        q = _mm(qa, w_uq, tm=tm, tn=1024, out_dtype=BF16, name="mla_up_q",
                epi=functools.partial(_mla_q_epilogue, scale=scale),
                extras=[(q_g, (1, MLA_HEAD_PAD), const)] + rope)
    return q, k, v


def _pad_heads(w, head_w):
    k = w.shape[0]
    w = w.reshape(k, MLA_HEADS, head_w)
    return jnp.pad(w, ((0, 0), (0, 0), (0, MLA_HEAD_PAD - head_w))).reshape(k, MLA_HEADS * MLA_HEAD_PAD)


def _mla_mixer(h_l, h_c, w_down, qa_g, kva_g, w_uq, w_ukv, q_g, k_g, w_o, rope_l, rope_c, need_ctx):
    w_dq = w_down[:, :MLA_Q_LORA].astype(BF16)
    w_dkv = jnp.pad(w_down[:, MLA_Q_LORA:], ((0, 0), (0, LANES - MLA_ROPE))).astype(BF16)
    w_uq_p = _pad_heads(w_uq, MLA_QK).astype(BF16)
    w_ukv_b = w_ukv.astype(BF16)
    qg = jnp.pad(q_g, (0, MLA_HEAD_PAD - MLA_QK)).reshape(1, MLA_HEAD_PAD)
    kg = jnp.pad(k_g, (0, MLA_HEAD_PAD - MLA_QK)).reshape(1, MLA_HEAD_PAD)
    proj = functools.partial(_mla_project, w_dq=w_dq, w_dkv=w_dkv, qa_g=qa_g, kva_g=kva_g, w_uq=w_uq_p,
                             w_ukv=w_ukv_b, q_g=qg, k_g=kg)
    q_l, k_l, v_l = proj(h_l, cos=rope_l[0], sin=rope_l[1], with_q=True)
    q_c, k_c, v_c = proj(h_c, cos=rope_c[0], sin=rope_c[1], with_q=need_ctx)
    y_l = _mla_attention(q_l, k_c, v_c, k_l, v_l)
    y_c = None
    if need_ctx:
        y_c = _attention(q_c, k_c, v_c, n_heads=MLA_HEADS, dq=MLA_HEAD_PAD, dv=MLA_V,
                         v_block=lambda h: 2 * h, base2=True)
    return y_l, y_c, w_o.astype(BF16)


def kernel(x, c, ctx, c_ctx, ada_w, ada_b, norm1_g, norm2_g, router_w, moe_w_gate, moe_w_up, moe_w_down, ev_w_in, ev_w_out, na_q_g, na_k_g, na_rpb, hy_short_w, hy_short_b, hy_w1, hy_b1, hy_w2, hy_b2, hy_w3, hy_freq, hy_d, mla_w_down, mla_qa_g, mla_kva_g, mla_w_uq, mla_w_ukv, mla_q_g, mla_k_g, mla_w_o):
    B, S, D = x.shape
    lc = ctx.shape[1]
    depth = ada_w.shape[0]
    assert B + 1 <= 8
    c8 = jnp.concatenate([c, c_ctx[None, :], jnp.zeros((8 - B - 1, D), F32)], axis=0)
    mods = _ada_modulation(c8, ada_w, ada_b)
    mats_l, mats_c = _dft_mats(S // _hy_blocks(S)), _dft_mats(lc // _hy_blocks(lc))
    rope_l = _rope_tables(S)
    rope_c = (jnp.ones((lc, LANES), F32), jnp.zeros((lc, LANES), F32))
    wg, wu, wd = (w.astype(BF16) for w in (moe_w_gate, moe_w_up, moe_w_down))
    for i in range(depth):
        last = i == depth - 1
        j = i // 2
        m_l = [mods[i, :B, None, k * D:(k + 1) * D] for k in range(N_MOD)]
        m_c = [jnp.broadcast_to(mods[i, B, None, None, k * D:(k + 1) * D], (B, 1, D)) for k in range(N_MOD)]
        h_l = _normmod(x, norm1_g[i], m_l[0], m_l[1])
        h_c = _normmod(ctx, norm1_g[i], m_c[0], m_c[1])
        if i % 2 == 0:
            filt = functools.partial(_hyena_filters, w1=hy_w1[j], b1=hy_b1[j], w2=hy_w2[j], b2=hy_b2[j],
                                     w3=hy_w3[j], freq=hy_freq[j])
            y_l, y_c, w_o = _even_mixer(
                h_l, h_c, ev_w_in[j].astype(BF16), ev_w_out[j].astype(BF16), na_q_g[j], na_k_g[j],
                _na_bias(na_rpb[j]), hy_short_w[j], hy_short_b[j], filt(S), filt(lc) if not last else None,
                hy_d[j], mats_l, mats_c, not last)
        else:
            y_l, y_c, w_o = _mla_mixer(h_l, h_c, mla_w_down[j], mla_qa_g[j], mla_kva_g[j], mla_w_uq[j],
                                       mla_w_ukv[j], mla_q_g[j], mla_k_g[j], mla_w_o[j], rope_l, rope_c,
                                       not last)
        x = _gated_residual_mm(y_l, w_o, x, m_l[2], "mixer_out_proj")
        if not last:
            ctx = _gated_residual_mm(y_c, w_o, ctx, m_c[2], "mixer_out_proj")
        x, ctx = _moe(i, x, None if last else ctx, norm2_g[i], m_l, m_c, router_w[i].T, wg, wu, wd)
    return x
```

```python
import functools
import math

import numpy as np
import jax
import jax.numpy as jnp
from jax import lax
from jax.experimental import pallas as pl
from jax.experimental.pallas import tpu as pltpu

F32, BF16, I32, U32 = jnp.float32, jnp.bfloat16, jnp.int32, jnp.uint32
HIGHEST = lax.Precision.HIGHEST

D_MODEL = 2048
DEPTH = 4
GRID_W = 64
N_MOD = 6
NA_HEADS = 8
NA_HEAD_DIM = 128
NA_WIDTH = NA_HEADS * NA_HEAD_DIM
NA_WIN_ROWS = 8
NA_WIN_COLS = 16
HY_WIDTH = D_MODEL - NA_WIDTH
HY_ORDER = 2
HY_EMB = 33
HY_BANDS = (HY_EMB - 1) // 2
HY_HIDDEN = 64
HY_DECAY_SHORT = 0.3
HY_DECAY_LONG = 1.5
HY_DECAY_TARGET = 1e-2
MLA_HEADS = 16
MLA_Q_LORA = 768
MLA_KV_LORA = 512
MLA_NOPE = 128
MLA_ROPE = 64
MLA_V = 128
MLA_QK = MLA_NOPE + MLA_ROPE
MLA_HEAD_PAD = 256
ROPE_BASE = 10000.0
N_EXPERTS = 16
EC_CAPACITY = 2
EXPERT_FF = 1024
EPS = 1e-6
NEG_INF = -1e30

LANES = 128
V7X_VMEM_LIMIT_BYTES = 56 * 1024 * 1024


def _params(n_axes):
    return pltpu.CompilerParams(dimension_semantics=("arbitrary",) * n_axes,
                                vmem_limit_bytes=V7X_VMEM_LIMIT_BYTES)


def _nt(a, b):
    return lax.dot_general(a, b, (((1,), (1,)), ((), ())), preferred_element_type=F32)


def _mm(a, b, *, tm, tn, out_dtype, epi=None, extras=(), n_out=1, order="gij", precision=None, name="mm"):
    a3, b3 = a.ndim == 3, b.ndim == 3
    G = a.shape[0] if a3 else (b.shape[0] if b3 else 1)
    M = a.shape[-2]
    K, N = b.shape[-2:]
    tm, tn = min(tm, M), min(tn, N)
    assert M % tm == 0 and N % tn == 0 and a.shape[-1] >= K
    sizes = dict(g=G, i=M // tm, j=N // tn)
    grid = tuple(sizes[c] for c in order)

    def gij(pid):
        d = dict(zip(order, pid))
        return d["g"], d["i"], d["j"]

    def a_map(*pid):
        g, i, _ = gij(pid)
        return (g, i, 0) if a3 else (i, 0)

    def b_map(*pid):
        g, _, j = gij(pid)
        return (g, 0, j) if b3 else (0, j)

    in_specs = [pl.BlockSpec((None, tm, K) if a3 else (tm, K), a_map),
                pl.BlockSpec((None, K, tn) if b3 else (K, tn), b_map)]
    for _, blk, fn in extras:
        in_specs.append(pl.BlockSpec(blk, lambda *pid, fn=fn: fn(*gij(pid))))

    def body(a_ref, b_ref, *rest):
        extra_refs, o_refs = rest[:len(extras)], rest[len(extras):]
        acc = jnp.dot(a_ref[...], b_ref[...], preferred_element_type=F32, precision=precision)
        outs = acc if epi is None else epi(acc, *extra_refs)
        for o_ref, o in zip(o_refs, outs if n_out > 1 else (outs,)):
            o_ref[...] = o.astype(o_ref.dtype)

    out = pl.pallas_call(
        body, grid=grid, in_specs=in_specs,
        out_specs=[pl.BlockSpec((None, tm, tn), lambda *pid: gij(pid))] * n_out,
        out_shape=[jax.ShapeDtypeStruct((G, M, N), out_dtype)] * n_out,
        compiler_params=_params(3), name=name,
    )(a, b, *[e[0] for e in extras])
    return out if n_out > 1 else out[0]


def _gated_residual_mm(a, w, res, gate, name):
    tm, tn = min(1024, a.shape[1]), min(1024, w.shape[1])
    return _mm(a, w, tm=tm, tn=tn, out_dtype=F32, name=name,
               epi=lambda acc, r_ref, g_ref: r_ref[...] + g_ref[...] * acc,
               extras=[(res, (None, tm, tn), lambda g, i, j: (g, i, j)),
                       (gate, (None, 1, tn), lambda g, i, j: (g, 0, j))])


def _ada_kernel(c_ref, w_ref, b_ref, o_ref):
    c = c_ref[...]
    sc = c * jax.nn.sigmoid(c)
    o_ref[...] = jnp.dot(sc, w_ref[...], preferred_element_type=F32, precision=HIGHEST) + b_ref[...]


def _ada_modulation(c8, ada_w, ada_b):
    depth, d, n = ada_w.shape
    tn = 1024
    return pl.pallas_call(
        _ada_kernel, grid=(depth, n // tn),
        in_specs=[pl.BlockSpec((8, d), lambda l, j: (0, 0)),
                  pl.BlockSpec((None, d, tn), lambda l, j: (l, 0, j)),
                  pl.BlockSpec((None, 1, tn), lambda l, j: (l, 0, j))],
        out_specs=pl.BlockSpec((None, 8, tn), lambda l, j: (l, 0, j)),
        out_shape=jax.ShapeDtypeStruct((depth, 8, n), F32),
        compiler_params=_params(2), name="ada_modulation",
    )(c8, ada_w, ada_b.reshape(depth, 1, n))


def _normmod_kernel(x_ref, g_ref, sh_ref, sc_ref, *rest, with_router):
    x = x_ref[...]
    y = x * lax.rsqrt(jnp.mean(x * x, axis=-1, keepdims=True) + EPS) * g_ref[...]
    h = y * (1.0 + sc_ref[...]) + sh_ref[...]
    if with_router:
        rw_ref, o_ref, lg_ref = rest
        lg_ref[...] = lax.dot_general(rw_ref[...], h, (((1,), (1,)), ((), ())),
                                      preferred_element_type=F32, precision=HIGHEST)
        o_ref[...] = _pack_bf16_pairs(h)
    else:
        (o_ref,) = rest
        o_ref[...] = h.astype(o_ref.dtype)


def _pack_bf16_pairs(h):
    half = h.shape[1] // 2
    bits = pltpu.bitcast(h.astype(BF16).astype(F32), U32)
    return (bits[:, :half] >> 16) | (bits[:, half:] & jnp.uint32(0xFFFF0000))


def _unpack_bf16_pairs(p):
    lo = pltpu.bitcast(p << 16, F32)
    hi = pltpu.bitcast(p & jnp.uint32(0xFFFF0000), F32)
    return jnp.concatenate([lo, hi], axis=1).astype(BF16)


def _normmod(x, g, shift, scale, router_wt=None):
    G, M, D = x.shape
    tm = min(512, M)
    with_router = router_wt is not None
    in_specs = [pl.BlockSpec((None, tm, D), lambda g_, i: (g_, i, 0)),
                pl.BlockSpec((1, D), lambda g_, i: (0, 0)),
                pl.BlockSpec((None, 1, D), lambda g_, i: (g_, 0, 0)),
                pl.BlockSpec((None, 1, D), lambda g_, i: (g_, 0, 0))]
    out_specs = [pl.BlockSpec((None, tm, D), lambda g_, i: (g_, i, 0))]
    out_shape = [jax.ShapeDtypeStruct((G, M, D), BF16)]
    args = [x, g.reshape(1, D), shift, scale]
    if with_router:
        E = router_wt.shape[0]
        out_specs = [pl.BlockSpec((None, tm, D // 2), lambda g_, i: (g_, i, 0))]
        out_shape = [jax.ShapeDtypeStruct((G, M, D // 2), U32)]
        in_specs.append(pl.BlockSpec((E, D), lambda g_, i: (0, 0)))
        out_specs.append(pl.BlockSpec((None, E, tm), lambda g_, i: (g_, 0, i)))
        out_shape.append(jax.ShapeDtypeStruct((G, E, M), F32))
        args.append(router_wt)
    out = pl.pallas_call(
        functools.partial(_normmod_kernel, with_router=with_router), grid=(G, M // tm),
        in_specs=in_specs, out_specs=out_specs, out_shape=out_shape,
        compiler_params=_params(2), name="normmod_router" if with_router else "normmod",
    )(*args)
    return out if with_router else out[0]


def _headnorm_kernel(x_ref, g_ref, o_ref, *, n_heads, head_dim, scale):
    for h in range(n_heads):
        sl = slice(h * head_dim, (h + 1) * head_dim)
        x = x_ref[:, sl].astype(F32)
        y = x * lax.rsqrt(jnp.mean(x * x, axis=-1, keepdims=True) + EPS) * g_ref[...]
        o_ref[:, sl] = (y * scale).astype(o_ref.dtype)


def _headnorm(x, lane_block, g, scale):
    G, M, _ = x.shape
    tm = min(512, M)
    return pl.pallas_call(
        functools.partial(_headnorm_kernel, n_heads=NA_HEADS, head_dim=NA_HEAD_DIM, scale=scale),
        grid=(G, M // tm),
        in_specs=[pl.BlockSpec((None, tm, NA_WIDTH), lambda g_, i: (g_, i, lane_block)),
                  pl.BlockSpec((1, NA_HEAD_DIM), lambda g_, i: (0, 0))],
        out_specs=pl.BlockSpec((None, tm, NA_WIDTH), lambda g_, i: (g_, i, 0)),
        out_shape=jax.ShapeDtypeStruct((G, M, NA_WIDTH), BF16),
        compiler_params=_params(2), name="headnorm",
    )(x, g.reshape(1, NA_HEAD_DIM))


def _attn_kernel(q_ref, k_ref, v_ref, o_ref, *, base2):
    s = _nt(q_ref[...], k_ref[...])
    d = s - jnp.max(s, axis=-1, keepdims=True)
    p = jnp.exp2(d) if base2 else jnp.exp(d)
    l = jnp.sum(p, axis=-1, keepdims=True)
    o = jnp.dot(p.astype(BF16), v_ref[...], preferred_element_type=F32)
    o_ref[...] = (o / l).astype(o_ref.dtype)


def _attention(q, k, v, *, n_heads, dq, dv, v_block, base2=False):
    G, L, _ = q.shape
    return pl.pallas_call(
        functools.partial(_attn_kernel, base2=base2), grid=(G, n_heads),
        in_specs=[pl.BlockSpec((None, L, dq), lambda g, h: (g, 0, h)),
                  pl.BlockSpec((None, L, dq), lambda g, h: (g, 0, h)),
                  pl.BlockSpec((None, L, dv), lambda g, h: (g, 0, v_block(h)))],
        out_specs=pl.BlockSpec((None, L, dv), lambda g, h: (g, 0, h)),
        out_shape=jax.ShapeDtypeStruct((G, L, n_heads * dv), BF16),
        compiler_params=_params(2), name="ctx_attention",
    )(q, k, v)


MLA_KEY_BLOCK = 512


def _mla_attn_kernel(q_ref, kc_ref, vc_ref, kl_ref, vl_ref, o_ref):
    q = q_ref[...]

    def probs(s, m):
        return jnp.exp2((s - m).astype(BF16))

    s = _nt(q, kc_ref[...])
    m = jnp.max(s, axis=-1, keepdims=True)
    acc = jnp.dot(probs(s, m), vc_ref[...], preferred_element_type=F32)
    for j in range(kl_ref.shape[0] // MLA_KEY_BLOCK):
        rows = slice(j * MLA_KEY_BLOCK, (j + 1) * MLA_KEY_BLOCK)
        s = _nt(q, kl_ref[rows, :])
        m_new = jnp.maximum(m, jnp.max(s, axis=-1, keepdims=True))
        acc = jnp.exp2(m - m_new) * acc + jnp.dot(probs(s, m_new), vl_ref[rows, :], preferred_element_type=F32)
        m = m_new
    o_ref[...] = (acc[:, :MLA_V] / acc[:, MLA_V:]).astype(o_ref.dtype)


def _mla_attention(q, k_c, v_c, k_l, v_l):
    G, S, _ = q.shape
    lc = k_c.shape[1]
    tq = min(1024, S)
    assert S % MLA_KEY_BLOCK == 0
    dq = MLA_HEAD_PAD
    head = lambda g, h, i: (g, 0, h)
    return pl.pallas_call(
        _mla_attn_kernel, grid=(G, MLA_HEADS, S // tq),
        in_specs=[pl.BlockSpec((None, tq, dq), lambda g, h, i: (g, i, h)),
                  pl.BlockSpec((None, lc, dq), head), pl.BlockSpec((None, lc, dq), head),
                  pl.BlockSpec((None, S, dq), head), pl.BlockSpec((None, S, dq), head)],
        out_specs=pl.BlockSpec((None, tq, MLA_V), lambda g, h, i: (g, i, h)),
        out_shape=jax.ShapeDtypeStruct((G, S, MLA_HEADS * MLA_V), BF16),
        compiler_params=_params(3), name="mla_attention",
    )(q, k_c, v_c, k_l, v_l)


NA_ROWS_PER_STEP = 16


def _na_kernel(q_ref, k_ref, v_ref, kc_ref, vc_ref, bias_ref, o_ref, *, n_rows):
    rb = pl.program_id(2)
    kc = kc_ref[...]
    vc = vc_ref[...]
    win = NA_WIN_ROWS * GRID_W
    q = q_ref[...]
    starts, s_w = [], []
    for i in range(NA_ROWS_PER_STEP):
        r = rb * NA_ROWS_PER_STEP + i
        r0 = jnp.clip(r - NA_WIN_ROWS // 2, 0, n_rows - NA_WIN_ROWS)
        starts.append(pl.multiple_of(r0 * GRID_W, GRID_W))
        s_w.append(_nt(q[i * GRID_W:(i + 1) * GRID_W, :], k_ref[pl.ds(starts[i], win), :]) + bias_ref[r - r0])
    s_w = jnp.concatenate(s_w, axis=0)
    s_c = _nt(q, kc)
    m = jnp.maximum(jnp.max(s_w, axis=-1, keepdims=True), jnp.max(s_c, axis=-1, keepdims=True))
    p_w = jnp.exp(s_w - m)
    p_c = jnp.exp(s_c - m)
    l = jnp.sum(p_w, axis=-1, keepdims=True) + jnp.sum(p_c, axis=-1, keepdims=True)
    p_w = p_w.astype(BF16)
    o_w = [jnp.dot(p_w[i * GRID_W:(i + 1) * GRID_W, :], v_ref[pl.ds(starts[i], win), :],
                   preferred_element_type=F32) for i in range(NA_ROWS_PER_STEP)]
    o = jnp.concatenate(o_w, axis=0) + jnp.dot(p_c.astype(BF16), vc, preferred_element_type=F32)
    o_ref[...] = (o / l).astype(o_ref.dtype)


def _na_attention(q, k, proj, kc, proj_c, bias):
    G, S, _ = q.shape
    lc = kc.shape[1]
    n_rows = S // GRID_W
    assert n_rows >= NA_WIN_ROWS and n_rows % NA_ROWS_PER_STEP == 0
    tq = NA_ROWS_PER_STEP * GRID_W
    hd = NA_HEAD_DIM
    v_off = 2 * NA_HEADS
    return pl.pallas_call(
        functools.partial(_na_kernel, n_rows=n_rows),
        grid=(G, NA_HEADS, n_rows // NA_ROWS_PER_STEP),
        in_specs=[pl.BlockSpec((None, tq, hd), lambda g, h, r: (g, r, h)),
                  pl.BlockSpec((None, S, hd), lambda g, h, r: (g, 0, h)),
                  pl.BlockSpec((None, S, hd), lambda g, h, r: (g, 0, v_off + h)),
                  pl.BlockSpec((None, lc, hd), lambda g, h, r: (g, 0, h)),
                  pl.BlockSpec((None, lc, hd), lambda g, h, r: (g, 0, v_off + h)),
                  pl.BlockSpec((None, NA_WIN_ROWS, GRID_W, NA_WIN_ROWS * GRID_W),
                               lambda g, h, r: (h, 0, 0, 0))],
        out_specs=pl.BlockSpec((None, tq, hd), lambda g, h, r: (g, r, h)),
        out_shape=jax.ShapeDtypeStruct((G, S, NA_WIDTH), BF16),
        compiler_params=_params(3), name="na_attention",
    )(q, k, proj, kc, proj_c, bias)


def _na_bias(rpb):
    cols = np.arange(GRID_W)
    col_start = np.clip(cols - NA_WIN_COLS // 2, 0, GRID_W - NA_WIN_COLS)
    in_win = (cols[None, :] >= col_start[:, None]) & (cols[None, :] < col_start[:, None] + NA_WIN_COLS)
    wr, wc = NA_WIN_ROWS, NA_WIN_COLS
    a = jnp.stack([rpb[:, wr - 1 - v:2 * wr - 1 - v, :] for v in range(wr)], axis=1).astype(F32)
    pad = GRID_W - wc
    ap = jnp.pad(a, ((0, 0), (0, 0), (0, 0), (pad, pad)))
    b = jnp.stack([ap[..., GRID_W - 1 - q:2 * GRID_W - 1 - q] for q in range(GRID_W)], axis=3)
    b = jnp.transpose(b, (0, 1, 3, 2, 4))
    b = jnp.where(in_win[None, None, :, None, :], b, NEG_INF)
    return b.reshape(rpb.shape[0], wr, GRID_W, wr * GRID_W)


def _shortconv_kernel(x_ref, w_ref, b_ref, o_ref):
    x = x_ref[...].astype(F32)
    n = x.shape[0]
    row = lax.broadcasted_iota(I32, x.shape, 0)
    prev = jnp.where(row == 0, 0.0, pltpu.roll(x, 1, 0))
    nxt = jnp.where(row == n - 1, 0.0, pltpu.roll(x, n - 1, 0))
    w = w_ref[...]
    o_ref[...] = (w[0:1] * prev + w[1:2] * x + w[2:3] * nxt + b_ref[...]).astype(o_ref.dtype)


def _shortconv(proj, lane_off, w, b):
    G, L, _ = proj.shape
    C = w.shape[1]
    tc = 256
    off = lane_off // tc
    return pl.pallas_call(
        _shortconv_kernel, grid=(G, C // tc),
        in_specs=[pl.BlockSpec((None, L, tc), lambda g, j: (g, 0, off + j)),
                  pl.BlockSpec((3, tc), lambda g, j: (0, j)),
                  pl.BlockSpec((1, tc), lambda g, j: (0, j))],
        out_specs=pl.BlockSpec((None, L, tc), lambda g, j: (g, 0, j)),
        out_shape=jax.ShapeDtypeStruct((G, L, C), BF16),
        compiler_params=_params(2), name="hyena_shortconv",
    )(proj, w, b.reshape(1, C))


def _hy_blocks(L):
    return 4 if L >= 1024 else 1


def _hyfilt_kernel(z_ref, zr_ref, w1_ref, b1_ref, w2_ref, b2_ref, fr_ref, w3f_ref, w3b_ref, dl_ref,
                   os_ref, od_ref, hid_s, hidr_s, *, n_blocks):
    @pl.when((pl.program_id(0) == 0) & (pl.program_id(1) == 0))
    def _():
        fr = fr_ref[...]
        for src, dst in ((z_ref, hid_s), (zr_ref, hidr_s)):
            hid = jnp.sin(fr * (jnp.dot(src[...], w1_ref[...], preferred_element_type=F32, precision=HIGHEST)
                                + b1_ref[...]))
            dst[...] = jnp.sin(fr * (jnp.dot(hid, w2_ref[...], preferred_element_type=F32, precision=HIGHEST)
                                     + b2_ref[...])).astype(dst.dtype)

    def filt(w3_ref, pos_ref, hid_ref):
        decay = jnp.exp(-pos_ref[:, 0:1] * dl_ref[...])
        f = jnp.dot(hid_ref[...], w3_ref[...].astype(BF16), preferred_element_type=F32) * decay
        return f * lax.rsqrt(jnp.sum(f * f, axis=0, keepdims=True) + EPS)

    L = z_ref.shape[0]
    P = L // n_blocks
    hf, hb = filt(w3f_ref, z_ref, hid_s), filt(w3b_ref, z_ref, hid_s)
    if n_blocks > 1:
        hf_rev, hb_rev = filt(w3f_ref, zr_ref, hidr_s), filt(w3b_ref, zr_ref, hidr_s)
    row0 = lax.broadcasted_iota(I32, (P, hf.shape[1]), 0) == 0
    for jj in range(2 * n_blocks - 1):
        j = jj - (n_blocks - 1)
        a = hf[j * P:(j + 1) * P] if j >= 0 else hb_rev[L + j * P:L + (j + 1) * P]
        b = hf_rev[L - j * P:L - (j - 1) * P] if j >= 1 else hb[-j * P:(1 - j) * P]
        if j == 0:
            a = a + jnp.where(row0, b, 0.0)
        b = jnp.where(row0, 0.0, b)
        os_ref[jj] = (a + b).astype(os_ref.dtype)
        od_ref[jj] = (a - b).astype(od_ref.dtype)


def _hyena_filters(L, w1, b1, w2, b2, w3, freq):
    t = jnp.linspace(0.0, 1.0, L, dtype=F32)[:, None]
    wv = 2.0 * math.pi * jnp.arange(L, dtype=F32)[:, None] / L
    bands = jnp.linspace(1e-4, HY_BANDS - 1, HY_BANDS, dtype=F32)[None, :]
    z = jnp.concatenate([t, jnp.cos(bands * wv), -jnp.sin(bands * wv)], axis=-1)
    z = jnp.pad(z, ((0, 0), (0, LANES - HY_EMB)))
    z_rev = jnp.roll(z[::-1], 1, axis=0)
    hp = LANES - HY_HIDDEN
    w1p = jnp.pad(w1, ((0, LANES - HY_EMB), (0, hp)))
    w2p = jnp.pad(w2, ((0, hp), (0, hp)))
    w3p = jnp.pad(w3, ((0, hp), (0, 0)))
    b1p = jnp.pad(b1, (0, hp)).reshape(1, LANES)
    b2p = jnp.pad(b2, (0, hp)).reshape(1, LANES)
    frp = jnp.pad(freq, (0, hp)).reshape(1, LANES)
    d_lo = math.log(HY_DECAY_TARGET) / HY_DECAY_LONG
    d_hi = math.log(HY_DECAY_TARGET) / HY_DECAY_SHORT
    deltas = jnp.abs(jnp.linspace(d_lo, d_hi, HY_WIDTH, dtype=F32)).reshape(1, HY_WIDTH)
    tc = 128
    nct = HY_WIDTH // tc
    n_blocks = _hy_blocks(L)
    n_lags, P = 2 * n_blocks - 1, L // n_blocks
    const = lambda o, j: (0, 0)
    out = jax.ShapeDtypeStruct((n_lags, P, HY_ORDER * HY_WIDTH), BF16)
    return pl.pallas_call(
        functools.partial(_hyfilt_kernel, n_blocks=n_blocks), grid=(HY_ORDER, nct),
        in_specs=[pl.BlockSpec((L, LANES), const), pl.BlockSpec((L, LANES), const),
                  pl.BlockSpec((LANES, LANES), const),
                  pl.BlockSpec((1, LANES), const), pl.BlockSpec((LANES, LANES), const),
                  pl.BlockSpec((1, LANES), const), pl.BlockSpec((1, LANES), const),
                  pl.BlockSpec((LANES, tc), lambda o, j: (0, 2 * o * nct + j)),
                  pl.BlockSpec((LANES, tc), lambda o, j: (0, (2 * o + 1) * nct + j)),
                  pl.BlockSpec((1, tc), lambda o, j: (0, j))],
        out_specs=[pl.BlockSpec((n_lags, P, tc), lambda o, j: (0, 0, o * nct + j))] * 2,
        out_shape=[out, out], scratch_shapes=[pltpu.VMEM((L, LANES), BF16), pltpu.VMEM((L, LANES), BF16)],
        compiler_params=_params(2), name="hyena_filters",
    )(z, z_rev, w1p, b1p, w2p, b2p, frp, w3p, w3p, deltas)


def _dft_tables(L):
    A = 1 << (int(math.log2(L)) // 2)
    period = 4 * L
    k = np.arange(L, dtype=np.int64)[:, None]
    ang_a = 2.0 * np.pi * (((2 * k + 1) * (A * np.arange(L // A, dtype=np.int64)[None, :])) % period) / period
    ang_b = 2.0 * np.pi * (((2 * k + 1) * np.arange(A, dtype=np.int64)[None, :]) % period) / period
    return tuple(np.asarray(f(x), np.float32) for x in (ang_a, ang_b) for f in (np.cos, np.sin))


def _dft_mats(L):
    ca, sa, cb, sb = (jnp.asarray(t) for t in _dft_tables(L))
    fc = (ca[:, :, None] * cb[:, None, :] - sa[:, :, None] * sb[:, None, :]).reshape(L, L)
    fs = (sa[:, :, None] * cb[:, None, :] + ca[:, :, None] * sb[:, None, :]).reshape(L, L)
    cat, sat, cbt, sbt = ca.T, sa.T, cb.T, sb.T
    fct = (cat[:, None, :] * cbt[None, :, :] - sat[:, None, :] * sbt[None, :, :]).reshape(L, L)
    fst = (sat[:, None, :] * cbt[None, :, :] + cat[:, None, :] * sbt[None, :, :]).reshape(L, L)
    return tuple(m.astype(BF16) for m in (fc, fs, fct, fst))


def _dft_fwd_kernel(fc_ref, fs_ref, u_ref, hc_ref, hs_ref, yc_ref, ys_ref, *, n_blocks):
    P = fc_ref.shape[0]
    fc, fs = fc_ref[...], fs_ref[...]
    xc, xs = [], []
    for i in range(n_blocks):
        u = u_ref[i * P:(i + 1) * P, :]
        xc.append(jnp.dot(fc, u, preferred_element_type=F32))
        xs.append(jnp.dot(fs, u, preferred_element_type=F32))
    for o in range(n_blocks):
        yc = ys = None
        for i in range(n_blocks):
            hc, hs = hc_ref[o - i + n_blocks - 1], hs_ref[o - i + n_blocks - 1]
            tc_, ts_ = xc[i] * hc - xs[i] * hs, xc[i] * hs + xs[i] * hc
            yc, ys = (tc_, ts_) if yc is None else (yc + tc_, ys + ts_)
        yc_ref[o * P:(o + 1) * P, :] = yc.astype(yc_ref.dtype)
        ys_ref[o * P:(o + 1) * P, :] = ys.astype(ys_ref.dtype)


def _dft_fwd(fc, fs, u, u_block, hc, hs, h_block):
    G, L, _ = u.shape
    n_lags, P, _ = hc.shape
    C = HY_WIDTH
    tc = 256
    nct = C // tc
    spec_f = pl.BlockSpec((P, P), lambda j, g: (0, 0))
    spec_h = pl.BlockSpec((n_lags, P, tc), lambda j, g: (0, 0, h_block * nct + j), pipeline_mode=pl.Buffered(1))
    spec_y = pl.BlockSpec((None, L, tc), lambda j, g: (g, 0, j))
    out = jax.ShapeDtypeStruct((G, L, C), BF16)
    return pl.pallas_call(
        functools.partial(_dft_fwd_kernel, n_blocks=(n_lags + 1) // 2), grid=(nct, G),
        in_specs=[spec_f, spec_f, pl.BlockSpec((None, L, tc), lambda j, g: (g, 0, u_block * nct + j)),
                  spec_h, spec_h],
        out_specs=[spec_y, spec_y], out_shape=[out, out],
        compiler_params=_params(2), name="hyena_dft_fwd",
    )(fc, fs, u, hc, hs)


def _dft_inv_kernel(fct_ref, fst_ref, yc_ref, ys_ref, u_ref, x_ref, d_ref, o_ref, *, inv_scale):
    y = (jnp.dot(fct_ref[...], yc_ref[...], preferred_element_type=F32)
         + jnp.dot(fst_ref[...], ys_ref[...], preferred_element_type=F32)) * inv_scale
    y = y + u_ref[...].astype(F32) * d_ref[...]
    o_ref[...] = (x_ref[...].astype(F32) * y).astype(o_ref.dtype)


def _dft_inv(fct, fst, yc, ys, u, u_block, gate, gate_block, d):
    G, L, C = yc.shape
    tr, tc = min(1024, L), 512
    nct = C // tc
    spec_f = pl.BlockSpec((tr, L), lambda n, g, j: (n, 0))
    spec_y = pl.BlockSpec((None, L, tc), lambda n, g, j: (g, 0, j))
    return pl.pallas_call(
        functools.partial(_dft_inv_kernel, inv_scale=1.0 / L),
        grid=(L // tr, G, nct),
        in_specs=[spec_f, spec_f, spec_y, spec_y,
                  pl.BlockSpec((None, tr, tc), lambda n, g, j: (g, n, u_block * nct + j)),
                  pl.BlockSpec((None, tr, tc), lambda n, g, j: (g, n, gate_block * nct + j)),
                  pl.BlockSpec((1, tc), lambda n, g, j: (0, j))],
        out_specs=pl.BlockSpec((None, tr, tc), lambda n, g, j: (g, n, j)),
        out_shape=jax.ShapeDtypeStruct((G, L, C), BF16),
        compiler_params=_params(3), name="hyena_dft_inv",
    )(fct, fst, yc, ys, u, gate, d.reshape(1, C))


def _hyena(proj, short_w, short_b, hsum, hdiff, hy_d, mats):
    fc, fs, fct, fst = mats
    G, L, _ = proj.shape
    n_blocks = (hsum.shape[0] + 1) // 2
    P = L // n_blocks
    blocks = lambda t: t.reshape(G * n_blocks, P, t.shape[-1])
    sc = _shortconv(proj, 3 * NA_WIDTH, short_w, short_b)
    hc = _mm(fc, hsum, tm=1024, tn=1024, out_dtype=F32, name="hyena_filter_dft")
    hs = _mm(fs, hdiff, tm=1024, tn=1024, out_dtype=F32, name="hyena_filter_dft")
    z = sc
    for o in range(HY_ORDER):
        yc, ys = _dft_fwd(fc, fs, z, 0, hc, hs, o)
        z = _dft_inv(fct, fst, blocks(yc), blocks(ys), blocks(z), 0, blocks(sc), o + 1, hy_d[o])
        z = z.reshape(G, L, HY_WIDTH)
    return z


def _rope_rotate(y):
    lane = lax.broadcasted_iota(I32, y.shape, 1)
    first = (lane % 32) < 16
    return jnp.where(first, -pltpu.roll(y, LANES - 16, 1), pltpu.roll(y, 16, 1))


def _mla_q_epilogue(acc, g_ref, cos_ref, sin_ref, *, scale):
    cos, sin = cos_ref[...], sin_ref[...]
    g = g_ref[...]
    pieces = []
    for h in range(acc.shape[1] // MLA_HEAD_PAD):
        lo = h * MLA_HEAD_PAD
        xn = acc[:, lo:lo + LANES]
        xr = acc[:, lo + LANES:lo + 2 * LANES]
        ss = jnp.sum(xn * xn + xr * xr, axis=-1, keepdims=True)
        rs = lax.rsqrt(ss * (1.0 / MLA_QK) + EPS) * scale
        yr = xr * rs * g[:, LANES:]
        pieces += [xn * rs * g[:, :LANES], yr * cos + _rope_rotate(yr) * sin]
    return jnp.concatenate(pieces, axis=1)


def _mla_kv_epilogue(acc, kr_ref, g_ref, cos_ref, sin_ref):
    cos, sin = cos_ref[...], sin_ref[...]
    g = g_ref[...]
    kr = kr_ref[...].astype(F32)
    sr = jnp.sum(kr * kr, axis=-1, keepdims=True)
    krg = kr * g[:, LANES:]
    krot = krg * cos + _rope_rotate(krg) * sin
    ones = jnp.ones((acc.shape[0], LANES), F32)
    ks, vs = [], []
    for h in range(acc.shape[1] // MLA_HEAD_PAD):
        lo = h * MLA_HEAD_PAD
        kn = acc[:, lo:lo + LANES]
        rs = lax.rsqrt((jnp.sum(kn * kn, axis=-1, keepdims=True) + sr) * (1.0 / MLA_QK) + EPS)
        ks += [kn * rs * g[:, :LANES], krot * rs]
        vs += [acc[:, lo + LANES:lo + 2 * LANES], ones]
    return jnp.concatenate(ks, axis=1), jnp.concatenate(vs, axis=1)


def _rope_tables(S):
    t = jnp.arange(S)
    half = MLA_ROPE // 2
    inv = ROPE_BASE ** (-jnp.arange(0, half, 2, dtype=F32) / half)
    ang_r = (t // GRID_W).astype(F32)[:, None] * inv[None, :]
    ang_c = (t % GRID_W).astype(F32)[:, None] * inv[None, :]
    ang = jnp.concatenate([ang_r, ang_r, ang_c, ang_c], axis=-1)
    pad = LANES - MLA_ROPE
    return (jnp.pad(jnp.cos(ang), ((0, 0), (0, pad)), constant_values=1.0),
            jnp.pad(jnp.sin(ang), ((0, 0), (0, pad))))


def _lane_cumsum(x):
    n = x.shape[1]
    tri = (lax.broadcasted_iota(I32, (LANES, LANES), 0) <= lax.broadcasted_iota(I32, (LANES, LANES), 1))
    tri = jnp.where(tri, 1.0, 0.0).astype(BF16)
    off = jnp.zeros((x.shape[0], 1), F32)
    outs = []
    for j in range(n // LANES):
        c = jnp.dot(x[:, j * LANES:(j + 1) * LANES], tri, preferred_element_type=F32) + off
        outs.append(c)
        off = c[:, LANES - 1:LANES]
    return jnp.concatenate(outs, axis=1)


ROUTE_SLOT_CHUNK = 64


def _route_kernel(lg_ref, idx_ref, gt_ref, aff_s, slot_s, *, cap, chunk):
    lg = lg_ref[...]
    n_exp, n_tok = lg.shape
    ex = jnp.exp(lg - jnp.max(lg, axis=0, keepdims=True))
    aff = ex / jnp.sum(ex, axis=0, keepdims=True)
    bits = pltpu.bitcast(aff, I32)

    def search(i, v):
        cand = v | jnp.left_shift(jnp.int32(1), 30 - i)
        cnt = jnp.sum(jnp.where(bits >= cand, 1.0, 0.0), axis=1, keepdims=True)
        return jnp.where(cnt >= cap, cand, v)

    thr = lax.fori_loop(0, 31, search, jnp.zeros((n_exp, 1), I32))
    gt = bits > thr
    eq = bits == thr
    need = cap - jnp.sum(jnp.where(gt, 1.0, 0.0), axis=1, keepdims=True)
    eq_rank = _lane_cumsum(jnp.where(eq, 1.0, 0.0).astype(BF16))
    sel = jnp.where(gt, 1.0, jnp.where(eq, jnp.where(eq_rank <= need, 1.0, 0.0), 0.0))
    slot = _lane_cumsum(sel.astype(BF16)) * sel
    aff_s[...] = aff
    slot_s[...] = slot
    tok = lax.broadcasted_iota(I32, (chunk, n_tok), 1).astype(F32)

    def per_expert(e, carry):
        srow = slot_s[pl.ds(e, 1), :]
        arow = aff_s[pl.ds(e, 1), :]

        def per_chunk(c, carry2):
            p0 = pl.multiple_of(c * chunk, chunk)
            want = (lax.broadcasted_iota(I32, (chunk, 1), 0) + (p0 + 1)).astype(F32)
            hit = srow == want
            idx_ref[e, pl.ds(p0, chunk), :] = jnp.sum(
                jnp.where(hit, tok, 0.0), axis=1, keepdims=True).astype(I32)
            gt_ref[e, pl.ds(p0, chunk), :] = jnp.sum(
                jnp.where(hit, arow, 0.0), axis=1, keepdims=True)
            return carry2

        return lax.fori_loop(0, cap // chunk, per_chunk, carry)

    lax.fori_loop(0, n_exp, per_expert, 0)


def _route(logits_t, cap):
    G, E, N = logits_t.shape
    chunk = min(ROUTE_SLOT_CHUNK, cap)
    assert cap % chunk == 0 and N % LANES == 0
    kern = functools.partial(_route_kernel, cap=cap, chunk=chunk)
    spec_o = pl.BlockSpec((None, E, cap, 1), lambda g: (g, 0, 0, 0))
    return pl.pallas_call(
        kern, grid=(G,), in_specs=[pl.BlockSpec((None, E, N), lambda g: (g, 0, 0))],
        out_specs=[spec_o, spec_o],
        out_shape=[jax.ShapeDtypeStruct((G, E, cap, 1), I32), jax.ShapeDtypeStruct((G, E, cap, 1), F32)],
        scratch_shapes=[pltpu.VMEM((E, N), F32), pltpu.VMEM((E, N), F32)],
        compiler_params=_params(1), name="moe_route",
    )(logits_t)


def _gather_kernel(idx_ref, h_ref, o_ref, *, cap, n_exp):
    base = (pl.program_id(0) * n_exp + pl.program_id(1)) * cap

    def body(p, carry):
        r = idx_ref[base + p]
        o_ref[pl.ds(p, 1), :] = h_ref[pl.ds(r, 1), :]
        return carry

    lax.fori_loop(0, cap, body, 0, unroll=8)


def _gather(idx_flat, h32, n_exp, cap):
    G, N, W = h32.shape
    return pl.pallas_call(
        functools.partial(_gather_kernel, cap=cap, n_exp=n_exp),
        grid_spec=pltpu.PrefetchScalarGridSpec(
            num_scalar_prefetch=1, grid=(G, n_exp),
            in_specs=[pl.BlockSpec((None, N, W), lambda g, e, idx: (g, 0, 0))],
            out_specs=pl.BlockSpec((None, None, cap, W), lambda g, e, idx: (g, e, 0, 0))),
        out_shape=jax.ShapeDtypeStruct((G, n_exp, cap, W), h32.dtype),
        compiler_params=_params(2), name="moe_gather",
    )(idx_flat, h32)


def _swiglu(x32, slot_gate, feat_gate, wg_ref, wu_ref, wd_ref):
    x = _unpack_bf16_pairs(x32)
    a = jnp.dot(x, wg_ref[...], preferred_element_type=F32)
    u = jnp.dot(x, wu_ref[...], preferred_element_type=F32)
    h = (a * jax.nn.sigmoid(a) * u).astype(BF16)
    return jnp.dot(h, wd_ref[...], preferred_element_type=F32) * slot_gate * feat_gate


def _ffn_kernel(*refs, n_lat, with_ctx):
    if with_ctx:
        xl_ref, gl_ref, fl_ref, xc_ref, gc_ref, fc_ref, wg_ref, wu_ref, wd_ref, yl_ref, yc_ref = refs
    else:
        xl_ref, gl_ref, fl_ref, wg_ref, wu_ref, wd_ref, yl_ref = refs
    step = pl.program_id(1)

    @pl.when(step < n_lat)
    def _():
        yl_ref[...] = _swiglu(xl_ref[...], gl_ref[...], fl_ref[...], wg_ref, wu_ref, wd_ref)

    if with_ctx:
        @pl.when(step == n_lat)
        def _():
            gb, cc, w = xc_ref.shape
            y = _swiglu(xc_ref[...].reshape(gb * cc, w), gc_ref[...].reshape(gb * cc, 1), fc_ref[...],
                        wg_ref, wu_ref, wd_ref)
            yc_ref[...] = y.reshape(gb, cc, y.shape[1])


def _expert_ffn(layer, wg, wu, wd, lat, ctx=None):
    xl, gl, fl = lat
    G, E, C, W = xl.shape
    D, FF = wg.shape[-2:]
    lat_slot = lambda e, s: (jnp.minimum(s, G - 1), e, 0, 0)
    w_idx = lambda e, s: (layer, e, 0, 0)
    in_specs = [pl.BlockSpec((None, None, C, W), lat_slot), pl.BlockSpec((None, None, C, 1), lat_slot),
                pl.BlockSpec((None, 1, D), lambda e, s: (jnp.minimum(s, G - 1), 0, 0))]
    out_specs = [pl.BlockSpec((None, None, C, D), lat_slot)]
    out_shape = [jax.ShapeDtypeStruct((G, E, C, D), F32)]
    args = [xl, gl, fl]
    if ctx is not None:
        xc, gc, fc = ctx
        Gc, _, Cc, _ = xc.shape
        ctx_slot = lambda e, s: (0, e, 0, 0)
        in_specs += [pl.BlockSpec((Gc, None, Cc, W), ctx_slot), pl.BlockSpec((Gc, None, Cc, 1), ctx_slot),
                     pl.BlockSpec((1, D), lambda e, s: (0, 0))]
        out_specs.append(pl.BlockSpec((Gc, None, Cc, D), ctx_slot))
        out_shape.append(jax.ShapeDtypeStruct((Gc, E, Cc, D), F32))
        args += [xc, gc, fc]
    in_specs += [pl.BlockSpec((None, None, D, FF), w_idx), pl.BlockSpec((None, None, D, FF), w_idx),
                 pl.BlockSpec((None, None, FF, D), w_idx)]
    return pl.pallas_call(
        functools.partial(_ffn_kernel, n_lat=G, with_ctx=ctx is not None),
        grid=(E, G + (ctx is not None)), in_specs=in_specs, out_specs=out_specs, out_shape=out_shape,
        compiler_params=_params(2), name="moe_expert_ffn",
    )(*args, wg, wu, wd)


COMBINE_ROWS = 8


def _combine_kernel(idx_ref, y_ref, x_ref, o_ref, *, cap, n_exp):
    e = pl.program_id(2)
    base = (pl.program_id(0) * n_exp + e) * cap

    @pl.when(e == 0)
    def _():
        o_ref[...] = x_ref[...]

    def body(i, carry):
        p0 = pl.multiple_of(i * COMBINE_ROWS, COMBINE_ROWS)
        rows = [idx_ref[base + p0 + k] for k in range(COMBINE_ROWS)]
        vals = [o_ref[pl.ds(rows[k], 1), :] + y_ref[pl.ds(p0 + k, 1), :] for k in range(COMBINE_ROWS)]
        for k in range(COMBINE_ROWS):
            o_ref[pl.ds(rows[k], 1), :] = vals[k]
        return carry

    lax.fori_loop(0, cap // COMBINE_ROWS, body, 0)


def _combine(idx_flat, y, x):
    G, E, C, D = y.shape
    N = x.shape[1]
    tc = 1024
    assert C % COMBINE_ROWS == 0
    return pl.pallas_call(
        functools.partial(_combine_kernel, cap=C, n_exp=E),
        grid_spec=pltpu.PrefetchScalarGridSpec(
            num_scalar_prefetch=1, grid=(G, D // tc, E),
            in_specs=[pl.BlockSpec((None, None, C, tc), lambda g, j, e, idx: (g, e, 0, j)),
                      pl.BlockSpec((None, N, tc), lambda g, j, e, idx: (g, 0, j),
                                   pipeline_mode=pl.Buffered(1))],
            out_specs=pl.BlockSpec((None, N, tc), lambda g, j, e, idx: (g, 0, j))),
        out_shape=jax.ShapeDtypeStruct((G, N, D), F32),
        compiler_params=_params(3), name="moe_combine",
    )(idx_flat, y, x)


def _route_and_gather(x, g2, shift, scale, router_wt):
    G, N, _ = x.shape
    E = router_wt.shape[0]
    cap = max(1, (EC_CAPACITY * N) // E)
    h32, logits_t = _normmod(x, g2, shift, scale, router_wt)
    idx, gates = _route(logits_t, cap)
    idx_flat = idx.reshape(G * E * cap)
    return idx_flat, gates, _gather(idx_flat, h32, E, cap)


def _moe(layer, x, ctx, g2, m_l, m_c, router_wt, wg, wu, wd):
    il, gl, xl = _route_and_gather(x, g2, m_l[3], m_l[4], router_wt)
    if ctx is None:
        (yl,) = _expert_ffn(layer, wg, wu, wd, (xl, gl, m_l[5]))
        return _combine(il, yl, x), None
    ic, gc, xc = _route_and_gather(ctx, g2, m_c[3], m_c[4], router_wt)
    yl, yc = _expert_ffn(layer, wg, wu, wd, (xl, gl, m_l[5]), (xc, gc, m_c[5][0]))
    return _combine(il, yl, x), _combine(ic, yc, ctx)


def _even_mixer(h_l, h_c, w_in, w_out, q_g, k_g, bias, short_w, short_b, filt_l, filt_c, hy_d,
                mats_l, mats_c, need_ctx):
    scale = NA_HEAD_DIM ** -0.5
    proj_l = _mm(h_l, w_in, tm=1024, tn=1024, out_dtype=BF16, name="even_in_proj")
    proj_c = _mm(h_c, w_in, tm=1024, tn=1024, out_dtype=BF16, name="even_in_proj")
    q_l = _headnorm(proj_l, 0, q_g, scale)
    k_l = _headnorm(proj_l, 1, k_g, 1.0)
    k_c = _headnorm(proj_c, 1, k_g, 1.0)
    na_l = _na_attention(q_l, k_l, proj_l, k_c, proj_c, bias)
    hy_l = _hyena(proj_l, short_w, short_b, *filt_l, hy_d, mats_l)
    y_l = jnp.concatenate([na_l, hy_l], axis=-1)
    y_c = None
    if need_ctx:
        q_c = _headnorm(proj_c, 0, q_g, scale)
        na_c = _attention(q_c, k_c, proj_c, n_heads=NA_HEADS, dq=NA_HEAD_DIM, dv=NA_HEAD_DIM,
                          v_block=lambda h: 2 * NA_HEADS + h)
        hy_c = _hyena(proj_c, short_w, short_b, *filt_c, hy_d, mats_c)
        y_c = jnp.concatenate([na_c, hy_c], axis=-1)
    return y_l, y_c, w_out


def _rms_rows(acc, g_ref, width):
    x = acc[:, :width]
    y = x * lax.rsqrt(jnp.mean(x * x, axis=-1, keepdims=True) + EPS) * g_ref[...]
    return y if width == acc.shape[1] else jnp.concatenate([y, acc[:, width:]], axis=1)


def _mla_project(h, w_dq, w_dkv, qa_g, kva_g, w_uq, w_ukv, q_g, k_g, cos, sin, with_q):
    scale = MLA_QK ** -0.5 * math.log2(math.e)
    kvw = w_dkv.shape[1]
    tm = min(1024, h.shape[1])
    const = lambda g, i, j: (0, 0)
    rope = [(cos, (tm, LANES), lambda g, i, j: (i, 0)), (sin, (tm, LANES), lambda g, i, j: (i, 0))]
    dkv = _mm(h, w_dkv, tm=tm, tn=kvw, out_dtype=BF16, name="mla_down_kv",
              epi=lambda acc, g_ref: _rms_rows(acc, g_ref, MLA_KV_LORA),
              extras=[(kva_g.reshape(1, -1), (1, MLA_KV_LORA), const)])
    kr_block = MLA_KV_LORA // LANES
    k, v = _mm(dkv, w_ukv, tm=tm, tn=1024, out_dtype=BF16, n_out=2, name="mla_up_kv", epi=_mla_kv_epilogue,
               extras=[(dkv, (None, tm, LANES), lambda g, i, j: (g, i, kr_block)),
                       (k_g, (1, MLA_HEAD_PAD), const)] + rope)
    q = None
    if with_q:
        qa = _mm(h, w_dq, tm=tm, tn=MLA_Q_LORA, out_dtype=BF16, name="mla_down_q",
                 epi=lambda acc, g_ref: _rms_rows(acc, g_ref, MLA_Q_LORA),
                 extras=[(qa_g.reshape(1, -1), (1, MLA_Q_LORA), const)])
        q = _mm(qa, w_uq, tm=tm, tn=1024, out_dtype=BF16, name="mla_up_q",
                epi=functools.partial(_mla_q_epilogue, scale=scale),
                extras=[(q_g, (1, MLA_HEAD_PAD), const)] + rope)
    return q, k, v


def _pad_heads(w, head_w):
    k = w.shape[0]
    w = w.reshape(k, MLA_HEADS, head_w)
    return jnp.pad(w, ((0, 0), (0, 0), (0, MLA_HEAD_PAD - head_w))).reshape(k, MLA_HEADS * MLA_HEAD_PAD)


def _mla_mixer(h_l, h_c, w_down, qa_g, kva_g, w_uq, w_ukv, q_g, k_g, w_o, rope_l, rope_c, need_ctx):
    w_dq = w_down[:, :MLA_Q_LORA].astype(BF16)
    w_dkv = jnp.pad(w_down[:, MLA_Q_LORA:], ((0, 0), (0, LANES - MLA_ROPE))).astype(BF16)
    w_uq_p = _pad_heads(w_uq, MLA_QK).astype(BF16)
    w_ukv_b = w_ukv.astype(BF16)
    qg = jnp.pad(q_g, (0, MLA_HEAD_PAD - MLA_QK)).reshape(1, MLA_HEAD_PAD)
    kg = jnp.pad(k_g, (0, MLA_HEAD_PAD - MLA_QK)).reshape(1, MLA_HEAD_PAD)
    proj = functools.partial(_mla_project, w_dq=w_dq, w_dkv=w_dkv, qa_g=qa_g, kva_g=kva_g, w_uq=w_uq_p,
                             w_ukv=w_ukv_b, q_g=qg, k_g=kg)
    q_l, k_l, v_l = proj(h_l, cos=rope_l[0], sin=rope_l[1], with_q=True)
    q_c, k_c, v_c = proj(h_c, cos=rope_c[0], sin=rope_c[1], with_q=need_ctx)
    y_l = _mla_attention(q_l, k_c, v_c, k_l, v_l)
    y_c = None
    if need_ctx:
        y_c = _attention(q_c, k_c, v_c, n_heads=MLA_HEADS, dq=MLA_HEAD_PAD, dv=MLA_V,
                         v_block=lambda h: 2 * h, base2=True)
    return y_l, y_c, w_o.astype(BF16)


def kernel(x, c, ctx, c_ctx, ada_w, ada_b, norm1_g, norm2_g, router_w, moe_w_gate, moe_w_up, moe_w_down, ev_w_in, ev_w_out, na_q_g, na_k_g, na_rpb, hy_short_w, hy_short_b, hy_w1, hy_b1, hy_w2, hy_b2, hy_w3, hy_freq, hy_d, mla_w_down, mla_qa_g, mla_kva_g, mla_w_uq, mla_w_ukv, mla_q_g, mla_k_g, mla_w_o):
    B, S, D = x.shape
    lc = ctx.shape[1]
    depth = ada_w.shape[0]
    assert B + 1 <= 8
    c8 = jnp.concatenate([c, c_ctx[None, :], jnp.zeros((8 - B - 1, D), F32)], axis=0)
    mods = _ada_modulation(c8, ada_w, ada_b)
    mats_l, mats_c = _dft_mats(S // _hy_blocks(S)), _dft_mats(lc // _hy_blocks(lc))
    rope_l = _rope_tables(S)
    rope_c = (jnp.ones((lc, LANES), F32), jnp.zeros((lc, LANES), F32))
    wg, wu, wd = (w.astype(BF16) for w in (moe_w_gate, moe_w_up, moe_w_down))
    for i in range(depth):
        last = i == depth - 1
        j = i // 2
        m_l = [mods[i, :B, None, k * D:(k + 1) * D] for k in range(N_MOD)]
        m_c = [jnp.broadcast_to(mods[i, B, None, None, k * D:(k + 1) * D], (B, 1, D)) for k in range(N_MOD)]
        h_l = _normmod(x, norm1_g[i], m_l[0], m_l[1])
        h_c = _normmod(ctx, norm1_g[i], m_c[0], m_c[1])
        if i % 2 == 0:
            filt = functools.partial(_hyena_filters, w1=hy_w1[j], b1=hy_b1[j], w2=hy_w2[j], b2=hy_b2[j],
                                     w3=hy_w3[j], freq=hy_freq[j])
            y_l, y_c, w_o = _even_mixer(
                h_l, h_c, ev_w_in[j].astype(BF16), ev_w_out[j].astype(BF16), na_q_g[j], na_k_g[j],
                _na_bias(na_rpb[j]), hy_short_w[j], hy_short_b[j], filt(S), filt(lc) if not last else None,
                hy_d[j], mats_l, mats_c, not last)
        else:
            y_l, y_c, w_o = _mla_mixer(h_l, h_c, mla_w_down[j], mla_qa_g[j], mla_kva_g[j], mla_w_uq[j],
                                       mla_w_ukv[j], mla_q_g[j], mla_k_g[j], mla_w_o[j], rope_l, rope_c,
                                       not last)
        x = _gated_residual_mm(y_l, w_o, x, m_l[2], "mixer_out_proj")
        if not last:
            ctx = _gated_residual_mm(y_c, w_o, ctx, m_c[2], "mixer_out_proj")
        x, ctx = _moe(i, x, None if last else ctx, norm2_g[i], m_l, m_c, router_w[i].T, wg, wu, wd)
    return x
```
